```python
import math
import jax, jax.numpy as jnp
from jax import lax
import numpy as np

D_MODEL = 1024
BATCH = 8
SEQ = 2048
DEPTH = 2
DEC_BATCH = 32
DEC_SEQ = 4
PAST_LEN = 16384
PAGE_SIZE = 128

D_MIX = D_MODEL
POOL_WIDTH = D_MIX // 2
POOL_WINDOWS = (2, 4, 8, 16)
N_POOL_GROUPS = len(POOL_WINDOWS)
POOL_GROUP = POOL_WIDTH // N_POOL_GROUPS
POOL_BUF = max(POOL_WINDOWS) - 1
ATTN_WIDTH = D_MIX - POOL_WIDTH
HEAD_DIM = 64
V_DIM = 2 * HEAD_DIM
N_HEADS = ATTN_WIDTH // V_DIM
ROT_DIM = HEAD_DIM // 4
ROPE_THETA = 500000.0
Q_BLOCK = 128
EPS = 1e-6
PROJ_SPLITS = (POOL_WIDTH, 2 * POOL_WIDTH, 2 * POOL_WIDTH + ATTN_WIDTH,
               2 * POOL_WIDTH + 2 * ATTN_WIDTH, 2 * POOL_WIDTH + 3 * ATTN_WIDTH)
PROJ_WIDTH = 2 * POOL_WIDTH + 4 * ATTN_WIDTH

kernel_name = "hybrid_pool_diffattn_decode_step"


def rms_norm(x, g):
    xf = x.astype(jnp.float32)
    y = xf * lax.rsqrt(jnp.mean(xf * xf, axis=-1, keepdims=True) + EPS)
    return (y * g.astype(jnp.float32)).astype(x.dtype)


def partial_rope(x, pos):
    half = ROT_DIM // 2
    inv = ROPE_THETA ** (-(jnp.arange(half, dtype=jnp.float32) * 2.0) / ROT_DIM)
    ang = pos.astype(jnp.float32)[:, None] * inv[None, :]
    cos = jnp.cos(ang)[None, :, None, None, :]
    sin = jnp.sin(ang)[None, :, None, None, :]
    x1 = x[..., :half].astype(jnp.float32)
    x2 = x[..., half:ROT_DIM].astype(jnp.float32)
    rot = jnp.concatenate([x1 * cos - x2 * sin, x1 * sin + x2 * cos], axis=-1).astype(x.dtype)
    return jnp.concatenate([rot, x[..., ROT_DIM:]], axis=-1)


def project(x, g_pre, w_in):
    B, S, _ = x.shape
    z = rms_norm(x, g_pre) @ w_in
    u, gp, q, k, v, ga = jnp.split(z, PROJ_SPLITS, axis=-1)
    q = q.reshape(B, S, N_HEADS, 2, HEAD_DIM)
    k = k.reshape(B, S, N_HEADS, 2, HEAD_DIM)
    v = v.reshape(B, S, N_HEADS, V_DIM)
    return u, gp, q, k, v, ga


def pool_mixer(rows, pos, w_mix, scale):
    B, L, _ = rows.shape
    g = rows.reshape(B, L, N_POOL_GROUPS, POOL_GROUP)
    c = jnp.cumsum(g.astype(jnp.float32), axis=1)
    means = []
    for gi, w in enumerate(POOL_WINDOWS):
        cg = c[:, :, gi]
        shifted = jnp.pad(cg, ((0, 0), (w, 0), (0, 0)))[:, :L]
        cnt = jnp.minimum(w, pos + 1).astype(jnp.float32)
        means.append((cg - shifted) / cnt[None, :, None])
    pooled = jnp.stack(means, axis=2) - g.astype(jnp.float32)
    mixed = jnp.einsum('blgc,gcd->blgd', pooled.astype(rows.dtype), w_mix)
    return mixed.reshape(B, L, POOL_WIDTH) * scale


def diff_attn_prompt(q, k, v, lam):
    B, S = q.shape[:2]
    nb = S // Q_BLOCK
    scale = HEAD_DIM ** -0.5
    qb = q.reshape(B, nb, Q_BLOCK, N_HEADS, 2, HEAD_DIM).transpose(1, 0, 2, 3, 4, 5)
    pos_b = jnp.arange(S).reshape(nb, Q_BLOCK)
    k_pos = jnp.arange(S)

    def one_block(args):
        qblk, qpos = args
        s = jnp.einsum('bqhcd,bkhcd->bhcqk', qblk, k).astype(jnp.float32) * scale
        mask = k_pos[None, :] <= qpos[:, None]
        s = jnp.where(mask[None, None, None], s, -jnp.inf)
        p = jax.nn.softmax(s, axis=-1)
        a = (p[:, :, 0] - lam * p[:, :, 1]).astype(v.dtype)
        return jnp.einsum('bhqk,bkhd->bqhd', a, v)

    o = lax.map(one_block, (qb, pos_b))
    return o.transpose(1, 0, 2, 3, 4).reshape(B, S, N_HEADS, V_DIM)


def diff_attn_sample(q, k_new, v_new, k_past, v_past, lam):
    DS = q.shape[1]
    P = k_past.shape[1]
    scale = HEAD_DIM ** -0.5
    s_past = jnp.einsum('bqhcd,bkhcd->bhcqk', q, k_past).astype(jnp.float32) * scale
    s_new = jnp.einsum('bqhcd,bkhcd->bhcqk', q, k_new).astype(jnp.float32) * scale
    causal = jnp.tril(jnp.ones((DS, DS), dtype=bool))
    s_new = jnp.where(causal[None, None, None], s_new, -jnp.inf)
    p = jax.nn.softmax(jnp.concatenate([s_past, s_new], axis=-1), axis=-1)
    a = (p[:, :, 0] - lam * p[:, :, 1]).astype(v_new.dtype)
    return (jnp.einsum('bhqk,bkhd->bqhd', a[..., :P], v_past)
            + jnp.einsum('bhqk,bkhd->bqhd', a[..., P:], v_new))


def finish(x, pool_out, gp, attn_out, ga, subln_g, lam_init, w_out, g_post):
    B, S, _ = x.shape
    attn_o = (rms_norm(attn_out, subln_g) * (1.0 - lam_init)).reshape(B, S, ATTN_WIDTH)
    mixed = jnp.concatenate([pool_out * jax.nn.silu(gp), attn_o * jax.nn.silu(ga)], axis=-1)
    return x + rms_norm(mixed @ w_out, g_post)


def setup_inputs(seed: int = 0) -> dict:
    key = jax.random.key(seed)
    ks = jax.random.split(key, 20)
    n_pages = PAST_LEN // PAGE_SIZE
    n_used = DEC_BATCH * n_pages
    n_phys = n_used + n_used // 4
    f32 = jnp.float32
    x_prompt = jax.random.normal(ks[0], (BATCH, SEQ, D_MODEL), f32)
    x_sample = jax.random.normal(ks[1], (DEC_BATCH, DEC_SEQ, D_MODEL), f32)
    cache_k = jax.random.normal(ks[2], (DEPTH, n_phys, PAGE_SIZE, N_HEADS, 2 * HEAD_DIM), f32)
    cache_v = jax.random.normal(ks[3], (DEPTH, n_phys, PAGE_SIZE, N_HEADS, V_DIM), f32)
    state_pool = jax.random.normal(ks[4], (DEPTH, DEC_BATCH, POOL_BUF, POOL_WIDTH), f32)
    page_table = jax.random.permutation(ks[5], n_phys)[:n_used].reshape(DEC_BATCH, n_pages).astype(jnp.int32)
    norm_pre = 1.0 + 0.1 * jax.random.normal(ks[6], (DEPTH, D_MODEL), f32)
    norm_post = 1.0 + 0.1 * jax.random.normal(ks[7], (DEPTH, D_MODEL), f32)
    w_in = jax.random.normal(ks[8], (DEPTH, D_MODEL, PROJ_WIDTH), f32) * D_MODEL ** -0.5
    w_pool_mix = jax.random.normal(ks[9], (DEPTH, N_POOL_GROUPS, POOL_GROUP, POOL_GROUP), f32) * POOL_GROUP ** -0.5
    pool_scale = 1.0 + 0.1 * jax.random.normal(ks[10], (DEPTH, POOL_WIDTH), f32)
    lambda_q1 = 0.1 * jax.random.normal(ks[11], (DEPTH, HEAD_DIM), f32)
    lambda_k1 = 0.1 * jax.random.normal(ks[12], (DEPTH, HEAD_DIM), f32)
    lambda_q2 = 0.1 * jax.random.normal(ks[13], (DEPTH, HEAD_DIM), f32)
    lambda_k2 = 0.1 * jax.random.normal(ks[14], (DEPTH, HEAD_DIM), f32)
    subln = 1.0 + 0.1 * jax.random.normal(ks[15], (DEPTH, V_DIM), f32)
    w_out = jax.random.normal(ks[16], (DEPTH, D_MIX, D_MODEL), f32) * D_MIX ** -0.5
    return {"x_prompt": x_prompt, "x_sample": x_sample, "cache_k": cache_k, "cache_v": cache_v,
            "state_pool": state_pool, "page_table": page_table, "norm_pre": norm_pre,
            "norm_post": norm_post, "w_in": w_in, "w_pool_mix": w_pool_mix, "pool_scale": pool_scale,
            "lambda_q1": lambda_q1, "lambda_k1": lambda_k1, "lambda_q2": lambda_q2, "lambda_k2": lambda_k2,
            "subln": subln, "w_out": w_out}


def reference(x_prompt, x_sample, cache_k, cache_v, state_pool, page_table, norm_pre, norm_post,
              w_in, w_pool_mix, pool_scale, lambda_q1, lambda_k1, lambda_q2, lambda_k2, subln, w_out):
    B, S, _ = x_prompt.shape
    DB, DS, _ = x_sample.shape
    pos_p = jnp.arange(S)
    pos_s = PAST_LEN + jnp.arange(DS)
    pos_rows_s = PAST_LEN - POOL_BUF + jnp.arange(POOL_BUF + DS)
    xp, xs = x_prompt, x_sample
    nk_p, nv_p, npool_p, nk_s, nv_s, npool_s = [], [], [], [], [], []
    for l in range(DEPTH):
        lam_init = 0.8 - 0.6 * math.exp(-0.3 * l)
        lam = (jnp.exp(jnp.sum(lambda_q1[l].astype(jnp.float32) * lambda_k1[l].astype(jnp.float32)))
               - jnp.exp(jnp.sum(lambda_q2[l].astype(jnp.float32) * lambda_k2[l].astype(jnp.float32)))
               + lam_init)
        u, gp, q, k, v, ga = project(xp, norm_pre[l], w_in[l])
        q = partial_rope(q, pos_p)
        k = partial_rope(k, pos_p)
        pool_out = pool_mixer(u, pos_p, w_pool_mix[l], pool_scale[l])
        attn_out = diff_attn_prompt(q, k, v, lam)
        xp = finish(xp, pool_out, gp, attn_out, ga, subln[l], lam_init, w_out[l], norm_post[l])
        nk_p.append(k.reshape(B, S, N_HEADS, 2 * HEAD_DIM))
        nv_p.append(v)
        npool_p.append(u[:, S - POOL_BUF:])
        u, gp, q, k, v, ga = project(xs, norm_pre[l], w_in[l])
        q = partial_rope(q, pos_s)
        k = partial_rope(k, pos_s)
        rows = jnp.concatenate([state_pool[l].astype(u.dtype), u], axis=1)
        pool_out = pool_mixer(rows, pos_rows_s, w_pool_mix[l], pool_scale[l])[:, POOL_BUF:]
        k_past = cache_k[l, page_table].reshape(DB, -1, N_HEADS, 2, HEAD_DIM)
        v_past = cache_v[l, page_table].reshape(DB, -1, N_HEADS, V_DIM)
        attn_out = diff_attn_sample(q, k, v, k_past, v_past, lam)
        xs = finish(xs, pool_out, gp, attn_out, ga, subln[l], lam_init, w_out[l], norm_post[l])
        nk_s.append(k.reshape(DB, DS, N_HEADS, 2 * HEAD_DIM))
        nv_s.append(v)
        npool_s.append(rows[:, DS:])
    new_k_prompt = jnp.stack(nk_p)
    new_v_prompt = jnp.stack(nv_p)
    new_pool_prompt = jnp.stack(npool_p)
    new_k_sample = jnp.stack(nk_s)
    new_v_sample = jnp.stack(nv_s)
    new_pool_sample = jnp.stack(npool_s)
    return (xp, xs, new_k_prompt, new_v_prompt, new_pool_prompt, new_k_sample, new_v_sample, new_pool_sample)
```

```python
import functools
import math

import jax
import jax.numpy as jnp
from jax import lax
from jax.experimental import pallas as pl
from jax.experimental.pallas import tpu as pltpu

F32 = jnp.float32
BF16 = jnp.bfloat16

POOL_WINDOWS = (2, 4, 8, 16)
POOL_GROUP = 128
POOL_WIDTH = POOL_GROUP * len(POOL_WINDOWS)
POOL_BUF = max(POOL_WINDOWS) - 1
HEAD_DIM = 64
V_DIM = 2 * HEAD_DIM
N_HEADS = 4
ATTN_WIDTH = N_HEADS * V_DIM
ROT_DIM = HEAD_DIM // 4
ROPE_THETA = 500000.0
EPS = 1e-6
PAGE_SIZE = 128
N_PROJ = 6

LANES = 128
SUBLANES = 8
HALO = 2 * SUBLANES
VMEM_LIMIT = 56 * 1024 * 1024

PROJ_ROWS = 512
ATTN_BLOCK = 256
DECODE_PAGES = 8


def _lam_init(layer):
    return 0.8 - 0.6 * math.exp(-0.3 * layer)


def _lam(lq1_ref, lk1_ref, lq2_ref, lk2_ref, lam_init):
    a = jnp.sum(lq1_ref[...] * lk1_ref[...], axis=-1, keepdims=True)
    b = jnp.sum(lq2_ref[...] * lk2_ref[...], axis=-1, keepdims=True)
    return jnp.exp(a) - jnp.exp(b) + lam_init


def _silu(x):
    return x * jax.nn.sigmoid(x)


def _rms(x, g):
    return x * lax.rsqrt(jnp.mean(x * x, axis=-1, keepdims=True) + EPS) * g


def _rope_tables(pos):
    half = ROT_DIM // 2
    inv = ROPE_THETA ** (-(jnp.arange(half, dtype=F32) * 2.0) / ROT_DIM)
    ang = pos.astype(F32)[:, None] * inv[None, :]
    cos, sin = jnp.cos(ang), jnp.sin(ang)
    n = pos.shape[0]
    rest = HEAD_DIM - ROT_DIM
    zeros_h = jnp.zeros((n, half), F32)
    c = jnp.concatenate([cos, cos, jnp.ones((n, rest), F32)], axis=1)
    a = jnp.concatenate([-sin, zeros_h, jnp.zeros((n, rest), F32)], axis=1)
    b = jnp.concatenate([zeros_h, sin, jnp.zeros((n, rest), F32)], axis=1)
    rep = LANES // HEAD_DIM
    return tuple(jnp.tile(t, (1, rep)) for t in (c, a, b))


def _proj_kernel(x_ref, g_ref, w_ref, c_ref, a_ref, b_ref,
                 u_ref, gp_ref, q_ref, k_ref, v_ref, ga_ref, kb_ref, vb_ref):
    xb = _rms(x_ref[...], g_ref[...]).astype(BF16)
    width = ATTN_WIDTH
    half = ROT_DIM // 2

    def mm(j):
        return jnp.dot(xb, w_ref[:, j * width:(j + 1) * width], preferred_element_type=F32)

    def rope(z, j):
        zc = z[:, j * LANES:(j + 1) * LANES]
        return (zc * c_ref[...] + pltpu.roll(zc, LANES - half, 1) * a_ref[...]
                + pltpu.roll(zc, half, 1) * b_ref[...])

    u_ref[...] = mm(0)
    gp_ref[...] = mm(1)
    q = mm(2)
    k = mm(3)
    for j in range(width // LANES):
        cols = slice(j * LANES, (j + 1) * LANES)
        q_ref[:, cols] = (rope(q, j) * HEAD_DIM ** -0.5).astype(BF16)
        kr = rope(k, j)
        k_ref[:, cols] = kr
        kb_ref[:, cols] = kr.astype(BF16)
    v = mm(4)
    v_ref[...] = v
    vb_ref[...] = v.astype(BF16)
    ga_ref[...] = mm(5)


def _project(x, g_pre, w_in, tables, rows):
    t, d = x.shape
    n_tab = tables[0].shape[0] // rows
    width = ATTN_WIDTH
    row_spec = lambda w: pl.BlockSpec((rows, w), lambda i: (i, 0))
    tab_spec = pl.BlockSpec((rows, LANES), lambda i: (i % n_tab, 0))
    f32_out = jax.ShapeDtypeStruct((t, width), F32)
    bf_out = jax.ShapeDtypeStruct((t, width), BF16)
    return pl.pallas_call(
        _proj_kernel,
        grid=(t // rows,),
        in_specs=[row_spec(d), pl.BlockSpec((1, d), lambda i: (0, 0)),
                  pl.BlockSpec((d, N_PROJ * width), lambda i: (0, 0)),
                  tab_spec, tab_spec, tab_spec],
        out_specs=[row_spec(width)] * 8,
        out_shape=[f32_out, f32_out, bf_out, f32_out, f32_out, f32_out, bf_out, bf_out],
        compiler_params=pltpu.CompilerParams(dimension_semantics=("parallel",),
                                             vmem_limit_bytes=VMEM_LIMIT),
        name="proj",
    )(x, g_pre.reshape(1, d), w_in, *tables)


def _attn_kernel(q_ref, k_ref, v_ref, ga_ref, sub_ref, lq1_ref, lk1_ref, lq2_ref, lk2_ref,
                 o_ref, *, blk, lam_init):
    qb = pl.program_id(2)
    q = q_ref[...]
    lane = lax.broadcasted_iota(jnp.int32, q.shape, 1)
    zero = jnp.zeros_like(q)
    qq = jnp.concatenate([jnp.where(lane < HEAD_DIM, q, zero),
                          jnp.where(lane >= HEAD_DIM, q, zero)], axis=0)

    def step(kb, carry, masked):
        m, l, acc = carry
        start = pl.multiple_of(kb * blk, blk)
        k = k_ref[pl.ds(start, blk), :]
        v = v_ref[pl.ds(start, blk), :]
        s = lax.dot_general(qq, k, (((1,), (1,)), ((), ())), preferred_element_type=F32)
        if masked:
            r = lax.broadcasted_iota(jnp.int32, s.shape, 0)
            c = lax.broadcasted_iota(jnp.int32, s.shape, 1)
            r = jnp.where(r >= blk, r - blk, r)
            s = jnp.where(c <= r, s, -jnp.inf)
        m_new = jnp.maximum(m, jnp.max(s, axis=-1, keepdims=True))
        alpha = jnp.exp(m - m_new)
        p = jnp.exp(s - m_new)
        l = alpha * l + jnp.sum(p, axis=-1, keepdims=True)
        acc = alpha * acc + jnp.dot(p.astype(BF16), v, preferred_element_type=F32)
        return m_new, l, acc

    init = (jnp.full((2 * blk, 1), -jnp.inf, F32), jnp.zeros((2 * blk, 1), F32),
            jnp.zeros((2 * blk, V_DIM), F32))
    carry = lax.fori_loop(0, qb, lambda kb, c: step(kb, c, False), init)
    _, l, acc = step(qb, carry, True)
    lam = _lam(lq1_ref, lk1_ref, lq2_ref, lk2_ref, lam_init)
    o = acc[:blk] / l[:blk] - lam * (acc[blk:] / l[blk:])
    y = _rms(o, sub_ref[...]) * (1.0 - lam_init)
    o_ref[...] = (y * _silu(ga_ref[...])).astype(BF16)


def _prompt_attention(q, kb, vb, ga, subln, lams, lam_init, batch, seq):
    t = q.shape[0]
    blk = ATTN_BLOCK
    nq = seq // blk
    q_spec = pl.BlockSpec((blk, V_DIM), lambda b, h, i: (b * nq + i, h))
    kv_spec = pl.BlockSpec((seq, V_DIM), lambda b, h, i: (b, h))
    vec = lambda n: pl.BlockSpec((1, n), lambda b, h, i: (0, 0))
    return pl.pallas_call(
        functools.partial(_attn_kernel, blk=blk, lam_init=lam_init),
        grid=(batch, N_HEADS, nq),
        in_specs=[q_spec, kv_spec, kv_spec, q_spec, vec(V_DIM)] + [vec(HEAD_DIM)] * 4,
        out_specs=q_spec,
        out_shape=jax.ShapeDtypeStruct((t, ATTN_WIDTH), BF16),
        compiler_params=pltpu.CompilerParams(
            dimension_semantics=("parallel", "parallel", "arbitrary"),
            vmem_limit_bytes=VMEM_LIMIT),
        name="prompt_attn",
    )(q, kb, vb, ga, subln.reshape(1, V_DIM), *lams)


def _mix_out(x, pooled, gp_ref, attn_part, wmix_ref, ps_ref, wout_ref, gpost_ref):
    parts = []
    for g, pg in enumerate(pooled):
        cols = slice(g * POOL_GROUP, (g + 1) * POOL_GROUP)
        mixed = jnp.dot(pg.astype(BF16), wmix_ref[g], preferred_element_type=F32) * ps_ref[:, cols]
        parts.append((mixed * _silu(gp_ref[:, cols])).astype(BF16))
    mixed_all = jnp.concatenate(parts + [attn_part], axis=1)
    z = jnp.dot(mixed_all, wout_ref[...], preferred_element_type=F32)
    return x + _rms(z, gpost_ref[...])


def _finish_kernel(x_ref, u_ref, halo_ref, gp_ref, am_ref, wmix_ref, ps_ref, wout_ref, gpost_ref,
                   y_ref, buf_ref, *, rows, blocks_per_seq):
    blk_in_seq = pl.program_id(0) % blocks_per_seq
    buf_ref[0:HALO, :] = jnp.where(blk_in_seq == 0, 0.0, halo_ref[...])
    buf_ref[HALO:, :] = u_ref[...]
    pos = lax.broadcasted_iota(jnp.int32, (rows, 1), 0) + blk_in_seq * rows
    pooled = []
    for g, w in enumerate(POOL_WINDOWS):
        cols = slice(g * POOL_GROUP, (g + 1) * POOL_GROUP)
        tok = buf_ref[HALO:HALO + rows, cols]
        acc = tok
        for j in range(1, w):
            acc = acc + buf_ref[HALO - j:HALO - j + rows, cols]
        cnt = jnp.minimum(w, pos + 1).astype(F32)
        pooled.append(acc / cnt - tok)
    y_ref[...] = _mix_out(x_ref[...], pooled, gp_ref, am_ref[...], wmix_ref, ps_ref, wout_ref,
                          gpost_ref)


def _prompt_finish(x, u, gp, am, w_mix, pool_scale, w_out, g_post, seq, rows):
    t, d = x.shape
    blocks_per_seq = seq // rows
    halo_per_block = rows // HALO
    row_spec = lambda w: pl.BlockSpec((rows, w), lambda i: (i, 0))
    halo_spec = pl.BlockSpec((HALO, POOL_WIDTH),
                             lambda i: (jnp.maximum(i * halo_per_block - 1, 0), 0))
    const = lambda shape: pl.BlockSpec(shape, lambda i: (0,) * len(shape))
    return pl.pallas_call(
        functools.partial(_finish_kernel, rows=rows, blocks_per_seq=blocks_per_seq),
        grid=(t // rows,),
        in_specs=[row_spec(d), row_spec(POOL_WIDTH), halo_spec, row_spec(POOL_WIDTH),
                  row_spec(ATTN_WIDTH), const(w_mix.shape), const((1, POOL_WIDTH)),
                  const(w_out.shape), const((1, d))],
        out_specs=row_spec(d),
        out_shape=jax.ShapeDtypeStruct((t, d), F32),
        scratch_shapes=[pltpu.VMEM((HALO + rows, POOL_WIDTH), F32)],
        compiler_params=pltpu.CompilerParams(dimension_semantics=("parallel",),
                                             vmem_limit_bytes=VMEM_LIMIT),
        name="prompt_finish",
    )(x, u, u, gp, am, w_mix, pool_scale.reshape(1, POOL_WIDTH), w_out, g_post.reshape(1, d))


def _sample_finish_kernel(x_ref, r_ref, gp_ref, ao_ref, ga_ref, sub_ref, wmix_ref, ps_ref,
                          wout_ref, gpost_ref, y_ref, *, n_new, counts, lam_init):
    pooled = []
    for g, w in enumerate(POOL_WINDOWS):
        cols = slice(g * POOL_GROUP, (g + 1) * POOL_GROUP)
        per_step = []
        for i in range(n_new):
            tok = r_ref[POOL_BUF + i, :, cols]
            acc = tok
            for j in range(1, w):
                acc = acc + r_ref[POOL_BUF + i - j, :, cols]
            per_step.append(acc / counts[g][i] - tok)
        pooled.append(jnp.concatenate(per_step, axis=0))
    heads = []
    for h in range(N_HEADS):
        cols = slice(h * V_DIM, (h + 1) * V_DIM)
        y = _rms(ao_ref[:, cols], sub_ref[...]) * (1.0 - lam_init)
        heads.append((y * _silu(ga_ref[:, cols])).astype(BF16))
    attn_part = jnp.concatenate(heads, axis=1)
    y_ref[...] = _mix_out(x_ref[...], pooled, gp_ref, attn_part, wmix_ref, ps_ref, wout_ref,
                          gpost_ref)


def _sample_finish(x, rows_hist, gp, ao, ga, subln, w_mix, pool_scale, w_out, g_post,
                   n_new, past_len, lam_init):
    t, d = x.shape
    counts = tuple(tuple(float(min(w, past_len + i + 1)) for i in range(n_new))
                   for w in POOL_WINDOWS)
    return pl.pallas_call(
        functools.partial(_sample_finish_kernel, n_new=n_new, counts=counts, lam_init=lam_init),
        out_shape=jax.ShapeDtypeStruct((t, d), F32),
        compiler_params=pltpu.CompilerParams(vmem_limit_bytes=VMEM_LIMIT),
        name="sample_finish",
    )(x, rows_hist, gp, ao, ga, subln.reshape(1, V_DIM), w_mix,
      pool_scale.reshape(1, POOL_WIDTH), w_out, g_post.reshape(1, d))


def _decode_kernel(pt_ref, q_ref, kn_ref, vn_ref, *rest, n_new, pages, lam_init):
    k_refs = rest[:pages]
    v_refs = rest[pages:2 * pages]
    lq1_ref, lk1_ref, lq2_ref, lk2_ref, o_ref, wq_ref, m_ref, l_ref, acc_ref, pad_ref = rest[2 * pages:]
    del pt_ref
    j = pl.program_id(1)
    groups = 2 * N_HEADS
    n_rows = n_new * groups
    nt = (((1,), (1,)), ((), ()))

    def update(s, vals):
        m = m_ref[...]
        m_new = jnp.maximum(m, jnp.max(s, axis=-1, keepdims=True))
        alpha = jnp.exp(m - m_new)
        p = jnp.exp(s - m_new)
        l_ref[...] = alpha * l_ref[...] + jnp.sum(p, axis=-1, keepdims=True)
        pv = None
        for r, v in enumerate(vals):
            t = jnp.dot(p[:, r * PAGE_SIZE:(r + 1) * PAGE_SIZE].astype(BF16), v,
                        preferred_element_type=F32)
            pv = t if pv is None else pv + t
        acc_ref[...] = alpha * acc_ref[...] + pv
        m_ref[...] = m_new

    @pl.when(j == 0)
    def _():
        q = q_ref[...]
        g_idx = lax.broadcasted_iota(jnp.int32, (groups, ATTN_WIDTH), 0)
        col = lax.broadcasted_iota(jnp.int32, (groups, ATTN_WIDTH), 1)
        sel = (col // HEAD_DIM) == g_idx
        wq = jnp.concatenate(
            [jnp.where(sel, jnp.broadcast_to(q[i:i + 1, :], (groups, ATTN_WIDTH)), 0.0)
             for i in range(n_new)], axis=0)
        wq_ref[...] = wq.astype(BF16)
        m_ref[...] = jnp.full(m_ref.shape, -jnp.inf, F32)
        l_ref[...] = jnp.zeros(l_ref.shape, F32)
        acc_ref[...] = jnp.zeros(acc_ref.shape, F32)
        pad_ref[...] = jnp.zeros(pad_ref.shape, F32)
        pad_ref[0:n_new, :] = kn_ref[...]
        s = lax.dot_general(wq_ref[...], pad_ref[...].astype(BF16), nt, preferred_element_type=F32)
        key = lax.broadcasted_iota(jnp.int32, s.shape, 1)
        qi = lax.broadcasted_iota(jnp.int32, s.shape, 0) // groups
        s = jnp.where(key <= qi, s, -jnp.inf)
        pad_ref[0:n_new, :] = vn_ref[...]
        update(s, [pad_ref[...].astype(BF16)])

    wq = wq_ref[...]
    s = jnp.concatenate(
        [lax.dot_general(wq, kr[...].astype(BF16), nt, preferred_element_type=F32) for kr in k_refs],
        axis=1)
    update(s, [vr[...].astype(BF16) for vr in v_refs])

    @pl.when(j == pl.num_programs(1) - 1)
    def _():
        lam = _lam(lq1_ref, lk1_ref, lq2_ref, lk2_ref, lam_init)
        n = acc_ref[...] / l_ref[...]
        g_idx = lax.broadcasted_iota(jnp.int32, (groups, ATTN_WIDTH), 0)
        col = lax.broadcasted_iota(jnp.int32, (groups, ATTN_WIDTH), 1)
        coef = jnp.where(col // V_DIM == g_idx // 2, jnp.where(g_idx % 2 == 0, 1.0, -lam), 0.0)
        for i in range(n_new):
            o_ref[i:i + 1, :] = jnp.sum(n[i * groups:(i + 1) * groups] * coef, axis=0, keepdims=True)


def _decode_attention(q, k_new, v_new, cache_k, cache_v, page_table, lams, layer, lam_init):
    db, n_new, width = q.shape
    n_pages = page_table.shape[1]
    pages = DECODE_PAGES
    groups = 2 * N_HEADS
    tok_spec = pl.BlockSpec((None, n_new, width), lambda b, j, pt: (b, 0, 0))

    def page_spec(r):
        return pl.BlockSpec((None, None, PAGE_SIZE, width),
                            lambda b, j, pt: (layer, pt[b * n_pages + j * pages + r], 0, 0))

    vec = pl.BlockSpec((1, HEAD_DIM), lambda b, j, pt: (0, 0))
    grid_spec = pltpu.PrefetchScalarGridSpec(
        num_scalar_prefetch=1,
        grid=(db, n_pages // pages),
        in_specs=[tok_spec] * 3 + [page_spec(r) for r in range(pages)] * 2 + [vec] * 4,
        out_specs=tok_spec,
        scratch_shapes=[pltpu.VMEM((n_new * groups, width), BF16),
                        pltpu.VMEM((n_new * groups, 1), F32),
                        pltpu.VMEM((n_new * groups, 1), F32),
                        pltpu.VMEM((n_new * groups, width), F32),
                        pltpu.VMEM((PAGE_SIZE, width), F32)],
    )
    return pl.pallas_call(
        functools.partial(_decode_kernel, n_new=n_new, pages=pages, lam_init=lam_init),
        grid_spec=grid_spec,
        out_shape=jax.ShapeDtypeStruct((db, n_new, width), F32),
        compiler_params=pltpu.CompilerParams(dimension_semantics=("parallel", "arbitrary"),
                                             vmem_limit_bytes=VMEM_LIMIT),
        name="decode_attn",
    )(page_table.reshape(-1), q, k_new, v_new, *([cache_k] * pages), *([cache_v] * pages), *lams)


def kernel(x_prompt, x_sample, cache_k, cache_v, state_pool, page_table, norm_pre, norm_post,
           w_in, w_pool_mix, pool_scale, lambda_q1, lambda_k1, lambda_q2, lambda_k2, subln, w_out):
    batch, seq, d = x_prompt.shape
    db, n_new, _ = x_sample.shape
    depth = w_in.shape[0]
    n_phys = cache_k.shape[1]
    past_len = page_table.shape[1] * PAGE_SIZE
    assert seq % PROJ_ROWS == 0 and seq % ATTN_BLOCK == 0 and page_table.shape[1] % DECODE_PAGES == 0
    assert w_in.shape[2] == N_PROJ * ATTN_WIDTH and POOL_WIDTH == ATTN_WIDTH

    w_in_b = w_in.astype(BF16)
    w_out_b = w_out.astype(BF16)
    w_mix_b = w_pool_mix.astype(BF16)
    cache_k = cache_k.reshape(depth, n_phys, PAGE_SIZE, ATTN_WIDTH)
    cache_v = cache_v.reshape(depth, n_phys, PAGE_SIZE, ATTN_WIDTH)
    tables_p = _rope_tables(jnp.arange(seq))
    tables_s = _rope_tables(past_len + jnp.repeat(jnp.arange(n_new), db))
    xp = x_prompt.reshape(batch * seq, d)
    xs = x_sample.transpose(1, 0, 2).reshape(n_new * db, d)

    def to_batch_major(a):
        return a.reshape(n_new, db, -1).transpose(1, 0, 2)

    outs = [[] for _ in range(6)]
    for layer in range(depth):
        lam_init = _lam_init(layer)
        lams = [v[layer].reshape(1, HEAD_DIM) for v in (lambda_q1, lambda_k1, lambda_q2, lambda_k2)]
        u, gp, q, k, v, ga, kb, vb = _project(xp, norm_pre[layer], w_in_b[layer], tables_p, PROJ_ROWS)
        am = _prompt_attention(q, kb, vb, ga, subln[layer], lams, lam_init, batch, seq)
        xp = _prompt_finish(xp, u, gp, am, w_mix_b[layer], pool_scale[layer], w_out_b[layer],
                            norm_post[layer], seq, PROJ_ROWS)
        outs[0].append(k.reshape(batch, seq, N_HEADS, V_DIM))
        outs[1].append(v.reshape(batch, seq, N_HEADS, V_DIM))
        outs[2].append(u.reshape(batch, seq, POOL_WIDTH)[:, seq - POOL_BUF:])
        u, gp, q, k, v, ga, _, _ = _project(xs, norm_pre[layer], w_in_b[layer], tables_s, n_new * db)
        ao = _decode_attention(to_batch_major(q.astype(F32)), to_batch_major(k), to_batch_major(v),
                               cache_k, cache_v, page_table, lams, layer, lam_init)
        hist = jnp.concatenate([state_pool[layer].transpose(1, 0, 2),
                                u.reshape(n_new, db, POOL_WIDTH)], axis=0)
        xs = _sample_finish(xs, hist, gp, ao.transpose(1, 0, 2).reshape(n_new * db, ATTN_WIDTH), ga,
                            subln[layer], w_mix_b[layer], pool_scale[layer], w_out_b[layer],
                            norm_post[layer], n_new, past_len, lam_init)
        outs[3].append(to_batch_major(k).reshape(db, n_new, N_HEADS, V_DIM))
        outs[4].append(to_batch_major(v).reshape(db, n_new, N_HEADS, V_DIM))
        outs[5].append(hist[n_new:].transpose(1, 0, 2))
    y_prompt = xp.reshape(batch, seq, d)
    y_sample = xs.reshape(n_new, db, d).transpose(1, 0, 2)
    return (y_prompt, y_sample) + tuple(jnp.stack(o) for o in outs)
```

```python
import functools
import math

import jax
import jax.numpy as jnp
from jax import lax
from jax.experimental import pallas as pl
from jax.experimental.pallas import tpu as pltpu

F32 = jnp.float32
BF16 = jnp.bfloat16

POOL_WINDOWS = (2, 4, 8, 16)
POOL_GROUP = 128
POOL_WIDTH = POOL_GROUP * len(POOL_WINDOWS)
POOL_BUF = max(POOL_WINDOWS) - 1
HEAD_DIM = 64
V_DIM = 2 * HEAD_DIM
N_HEADS = 4
ATTN_WIDTH = N_HEADS * V_DIM
ROT_DIM = HEAD_DIM // 4
ROPE_THETA = 500000.0
EPS = 1e-6
PAGE_SIZE = 128
N_PROJ = 6

LANES = 128
SUBLANES = 8
HALO = 2 * SUBLANES
VMEM_LIMIT = 56 * 1024 * 1024

PROJ_ROWS = 512
ATTN_BLOCK = 512
DECODE_PAGES = 8


def _lam_init(layer):
    return 0.8 - 0.6 * math.exp(-0.3 * layer)


def _lam(lq1_ref, lk1_ref, lq2_ref, lk2_ref, lam_init):
    a = jnp.sum(lq1_ref[...] * lk1_ref[...], axis=-1, keepdims=True)
    b = jnp.sum(lq2_ref[...] * lk2_ref[...], axis=-1, keepdims=True)
    return jnp.exp(a) - jnp.exp(b) + lam_init


def _silu(x):
    return x * jax.nn.sigmoid(x)


def _rms(x, g):
    return x * lax.rsqrt(jnp.mean(x * x, axis=-1, keepdims=True) + EPS) * g


def _rope_tables(pos):
    half = ROT_DIM // 2
    inv = ROPE_THETA ** (-(jnp.arange(half, dtype=F32) * 2.0) / ROT_DIM)
    ang = pos.astype(F32)[:, None] * inv[None, :]
    cos, sin = jnp.cos(ang), jnp.sin(ang)
    n = pos.shape[0]
    rest = HEAD_DIM - ROT_DIM
    zeros_h = jnp.zeros((n, half), F32)
    c = jnp.concatenate([cos, cos, jnp.ones((n, rest), F32)], axis=1)
    a = jnp.concatenate([-sin, zeros_h, jnp.zeros((n, rest), F32)], axis=1)
    b = jnp.concatenate([zeros_h, sin, jnp.zeros((n, rest), F32)], axis=1)
    rep = LANES // HEAD_DIM
    return tuple(jnp.tile(t, (1, rep)) for t in (c, a, b))


def _proj_kernel(x_ref, g_ref, w_ref, c_ref, a_ref, b_ref,
                 u_ref, gp_ref, q_ref, k_ref, v_ref, ga_ref, kb_ref, vb_ref):
    xb = _rms(x_ref[...], g_ref[...]).astype(BF16)
    width = ATTN_WIDTH
    half = ROT_DIM // 2

    def mm(j):
        return jnp.dot(xb, w_ref[:, j * width:(j + 1) * width], preferred_element_type=F32)

    def rope(z, j):
        zc = z[:, j * LANES:(j + 1) * LANES]
        return (zc * c_ref[...] + pltpu.roll(zc, LANES - half, 1) * a_ref[...]
                + pltpu.roll(zc, half, 1) * b_ref[...])

    u_ref[...] = mm(0)
    gp_ref[...] = mm(1)
    q = mm(2)
    k = mm(3)
    for j in range(width // LANES):
        cols = slice(j * LANES, (j + 1) * LANES)
        q_ref[:, cols] = (rope(q, j) * HEAD_DIM ** -0.5).astype(BF16)
        kr = rope(k, j)
        k_ref[:, cols] = kr
        kb_ref[:, cols] = kr.astype(BF16)
    v = mm(4)
    v_ref[...] = v
    vb_ref[...] = v.astype(BF16)
    ga_ref[...] = mm(5)


def _project(x, g_pre, w_in, tables, rows):
    t, d = x.shape
    n_tab = tables[0].shape[0] // rows
    width = ATTN_WIDTH
    row_spec = lambda w: pl.BlockSpec((rows, w), lambda i: (i, 0))
    tab_spec = pl.BlockSpec((rows, LANES), lambda i: (i % n_tab, 0))
    f32_out = jax.ShapeDtypeStruct((t, width), F32)
    bf_out = jax.ShapeDtypeStruct((t, width), BF16)
    return pl.pallas_call(
        _proj_kernel,
        grid=(t // rows,),
        in_specs=[row_spec(d), pl.BlockSpec((1, d), lambda i: (0, 0)),
                  pl.BlockSpec((d, N_PROJ * width), lambda i: (0, 0)),
                  tab_spec, tab_spec, tab_spec],
        out_specs=[row_spec(width)] * 8,
        out_shape=[f32_out, f32_out, bf_out, f32_out, f32_out, f32_out, bf_out, bf_out],
        compiler_params=pltpu.CompilerParams(dimension_semantics=("parallel",),
                                             vmem_limit_bytes=VMEM_LIMIT),
        name="proj",
    )(x, g_pre.reshape(1, d), w_in, *tables)


def _attn_kernel(q_ref, k_ref, v_ref, ga_ref, sub_ref, lq1_ref, lk1_ref, lq2_ref, lk2_ref,
                 o_ref, *, blk, lam_init):
    qb = pl.program_id(2)
    q = q_ref[...]
    lane = lax.broadcasted_iota(jnp.int32, q.shape, 1)
    zero = jnp.zeros_like(q)
    q_halves = (jnp.where(lane < HEAD_DIM, q, zero), jnp.where(lane >= HEAD_DIM, q, zero))

    def step(kb, carry, masked):
        start = pl.multiple_of(kb * blk, blk)
        k = k_ref[pl.ds(start, blk), :]
        v = v_ref[pl.ds(start, blk), :]
        out = []
        for qh, (m, l, acc) in zip(q_halves, carry):
            s = lax.dot_general(qh, k, (((1,), (1,)), ((), ())), preferred_element_type=F32)
            if masked:
                r = lax.broadcasted_iota(jnp.int32, s.shape, 0)
                c = lax.broadcasted_iota(jnp.int32, s.shape, 1)
                s = jnp.where(c <= r, s, -jnp.inf)
            m_new = jnp.maximum(m, jnp.max(s, axis=-1, keepdims=True))
            alpha = jnp.exp(m - m_new)
            p = jnp.exp(s - m_new)
            l = alpha * l + jnp.sum(p, axis=-1, keepdims=True)
            acc = alpha * acc + jnp.dot(p.astype(BF16), v, preferred_element_type=F32)
            out.append((m_new, l, acc))
        return tuple(out)

    init = (jnp.full((blk, 1), -jnp.inf, F32), jnp.zeros((blk, 1), F32),
            jnp.zeros((blk, V_DIM), F32))
    carry = lax.fori_loop(0, qb, lambda kb, c: step(kb, c, False), (init, init))
    (_, l1, acc1), (_, l2, acc2) = step(qb, carry, True)
    lam = _lam(lq1_ref, lk1_ref, lq2_ref, lk2_ref, lam_init)
    o = acc1 / l1 - lam * (acc2 / l2)
    y = _rms(o, sub_ref[...]) * (1.0 - lam_init)
    o_ref[...] = (y * _silu(ga_ref[...])).astype(BF16)


def _prompt_attention(q, kb, vb, ga, subln, lams, lam_init, batch, seq):
    t = q.shape[0]
    blk = ATTN_BLOCK
    nq = seq // blk
    q_spec = pl.BlockSpec((blk, V_DIM), lambda b, h, i: (b * nq + i, h))
    kv_spec = pl.BlockSpec((seq, V_DIM), lambda b, h, i: (b, h))
    vec = lambda n: pl.BlockSpec((1, n), lambda b, h, i: (0, 0))
    return pl.pallas_call(
        functools.partial(_attn_kernel, blk=blk, lam_init=lam_init),
        grid=(batch, N_HEADS, nq),
        in_specs=[q_spec, kv_spec, kv_spec, q_spec, vec(V_DIM)] + [vec(HEAD_DIM)] * 4,
        out_specs=q_spec,
        out_shape=jax.ShapeDtypeStruct((t, ATTN_WIDTH), BF16),
        compiler_params=pltpu.CompilerParams(
            dimension_semantics=("parallel", "parallel", "arbitrary"),
            vmem_limit_bytes=VMEM_LIMIT),
        name="prompt_attn",
    )(q, kb, vb, ga, subln.reshape(1, V_DIM), *lams)


def _mix_out(x, pooled, gp_ref, attn_part, wmix_ref, ps_ref, wout_ref, gpost_ref):
    parts = []
    for g, pg in enumerate(pooled):
        cols = slice(g * POOL_GROUP, (g + 1) * POOL_GROUP)
        mixed = jnp.dot(pg.astype(BF16), wmix_ref[g], preferred_element_type=F32) * ps_ref[:, cols]
        parts.append((mixed * _silu(gp_ref[:, cols])).astype(BF16))
    mixed_all = jnp.concatenate(parts + [attn_part], axis=1)
    z = jnp.dot(mixed_all, wout_ref[...], preferred_element_type=F32)
    return x + _rms(z, gpost_ref[...])


def _finish_kernel(x_ref, u_ref, halo_ref, gp_ref, am_ref, wmix_ref, ps_ref, wout_ref, gpost_ref,
                   y_ref, buf_ref, *, rows, blocks_per_seq):
    blk_in_seq = pl.program_id(0) % blocks_per_seq
    buf_ref[0:HALO, :] = jnp.where(blk_in_seq == 0, 0.0, halo_ref[...])
    buf_ref[HALO:, :] = u_ref[...]
    pos = lax.broadcasted_iota(jnp.int32, (rows, 1), 0) + blk_in_seq * rows
    pooled = []
    for g, w in enumerate(POOL_WINDOWS):
        cols = slice(g * POOL_GROUP, (g + 1) * POOL_GROUP)
        tok = buf_ref[HALO:HALO + rows, cols]
        acc = tok
        for j in range(1, w):
            acc = acc + buf_ref[HALO - j:HALO - j + rows, cols]
        cnt = jnp.minimum(w, pos + 1).astype(F32)
        pooled.append(acc / cnt - tok)
    y_ref[...] = _mix_out(x_ref[...], pooled, gp_ref, am_ref[...], wmix_ref, ps_ref, wout_ref,
                          gpost_ref)


def _prompt_finish(x, u, gp, am, w_mix, pool_scale, w_out, g_post, seq, rows):
    t, d = x.shape
    blocks_per_seq = seq // rows
    halo_per_block = rows // HALO
    row_spec = lambda w: pl.BlockSpec((rows, w), lambda i: (i, 0))
    halo_spec = pl.BlockSpec((HALO, POOL_WIDTH),
                             lambda i: (jnp.maximum(i * halo_per_block - 1, 0), 0))
    const = lambda shape: pl.BlockSpec(shape, lambda i: (0,) * len(shape))
    return pl.pallas_call(
        functools.partial(_finish_kernel, rows=rows, blocks_per_seq=blocks_per_seq),
        grid=(t // rows,),
        in_specs=[row_spec(d), row_spec(POOL_WIDTH), halo_spec, row_spec(POOL_WIDTH),
                  row_spec(ATTN_WIDTH), const(w_mix.shape), const((1, POOL_WIDTH)),
                  const(w_out.shape), const((1, d))],
        out_specs=row_spec(d),
        out_shape=jax.ShapeDtypeStruct((t, d), F32),
        scratch_shapes=[pltpu.VMEM((HALO + rows, POOL_WIDTH), F32)],
        compiler_params=pltpu.CompilerParams(dimension_semantics=("parallel",),
                                             vmem_limit_bytes=VMEM_LIMIT),
        name="prompt_finish",
    )(x, u, u, gp, am, w_mix, pool_scale.reshape(1, POOL_WIDTH), w_out, g_post.reshape(1, d))


def _sample_finish_kernel(x_ref, r_ref, gp_ref, ao_ref, ga_ref, sub_ref, wmix_ref, ps_ref,
                          wout_ref, gpost_ref, y_ref, *, n_new, counts, lam_init):
    pooled = []
    for g, w in enumerate(POOL_WINDOWS):
        cols = slice(g * POOL_GROUP, (g + 1) * POOL_GROUP)
        per_step = []
        for i in range(n_new):
            tok = r_ref[POOL_BUF + i, :, cols]
            acc = tok
            for j in range(1, w):
                acc = acc + r_ref[POOL_BUF + i - j, :, cols]
            per_step.append(acc / counts[g][i] - tok)
        pooled.append(jnp.concatenate(per_step, axis=0))
    heads = []
    for h in range(N_HEADS):
        cols = slice(h * V_DIM, (h + 1) * V_DIM)
        y = _rms(ao_ref[:, cols], sub_ref[...]) * (1.0 - lam_init)
        heads.append((y * _silu(ga_ref[:, cols])).astype(BF16))
    attn_part = jnp.concatenate(heads, axis=1)
    y_ref[...] = _mix_out(x_ref[...], pooled, gp_ref, attn_part, wmix_ref, ps_ref, wout_ref,
                          gpost_ref)


def _sample_finish(x, rows_hist, gp, ao, ga, subln, w_mix, pool_scale, w_out, g_post,
                   n_new, past_len, lam_init):
    t, d = x.shape
    counts = tuple(tuple(float(min(w, past_len + i + 1)) for i in range(n_new))
                   for w in POOL_WINDOWS)
    return pl.pallas_call(
        functools.partial(_sample_finish_kernel, n_new=n_new, counts=counts, lam_init=lam_init),
        out_shape=jax.ShapeDtypeStruct((t, d), F32),
        compiler_params=pltpu.CompilerParams(vmem_limit_bytes=VMEM_LIMIT),
        name="sample_finish",
    )(x, rows_hist, gp, ao, ga, subln.reshape(1, V_DIM), w_mix,
      pool_scale.reshape(1, POOL_WIDTH), w_out, g_post.reshape(1, d))


def _decode_kernel(pt_ref, q_ref, kn_ref, vn_ref, *rest, n_new, pages, lam_init):
    k_refs = rest[:pages]
    v_refs = rest[pages:2 * pages]
    lq1_ref, lk1_ref, lq2_ref, lk2_ref, o_ref, wq_ref, m_ref, l_ref, acc_ref, pad_ref = rest[2 * pages:]
    del pt_ref
    j = pl.program_id(1)
    qh = n_new * N_HEADS
    page_rows = PAGE_SIZE * N_HEADS
    nt = (((1,), (1,)), ((), ()))
    row = lax.broadcasted_iota(jnp.int32, (2 * qh, page_rows), 0)
    col = lax.broadcasted_iota(jnp.int32, (2 * qh, page_rows), 1)
    same_head = (row % N_HEADS) == (col % N_HEADS)

    def update(scores, vals):
        m = m_ref[...]
        m_new = m
        for s in scores:
            m_new = jnp.maximum(m_new, jnp.max(s, axis=-1, keepdims=True))
        alpha = jnp.exp(m - m_new)
        l = alpha * l_ref[...]
        acc = alpha * acc_ref[...]
        for s, v in zip(scores, vals):
            p = jnp.exp(s - m_new)
            l = l + jnp.sum(p, axis=-1, keepdims=True)
            acc = acc + jnp.dot(p.astype(BF16), v, preferred_element_type=F32)
        l_ref[...] = l
        acc_ref[...] = acc
        m_ref[...] = m_new

    @pl.when(j == 0)
    def _():
        q = q_ref[...]
        lane = lax.broadcasted_iota(jnp.int32, q.shape, 1)
        wq_ref[...] = jnp.concatenate([jnp.where(lane < HEAD_DIM, q, 0.0),
                                       jnp.where(lane >= HEAD_DIM, q, 0.0)], axis=0).astype(BF16)
        m_ref[...] = jnp.full(m_ref.shape, -jnp.inf, F32)
        l_ref[...] = jnp.zeros(l_ref.shape, F32)
        acc_ref[...] = jnp.zeros(acc_ref.shape, F32)
        pad_ref[...] = jnp.zeros(pad_ref.shape, F32)
        pad_ref[0:qh, :] = kn_ref[...]
        s = lax.dot_general(wq_ref[...], pad_ref[...].astype(BF16), nt, preferred_element_type=F32)
        causal = (col // N_HEADS) <= (row % qh) // N_HEADS
        s = jnp.where(same_head, jnp.where(causal, s, -jnp.inf), -jnp.inf)
        pad_ref[0:qh, :] = vn_ref[...]
        update([s], [pad_ref[...].astype(BF16)])

    wq = wq_ref[...]
    scores = [jnp.where(same_head,
                        lax.dot_general(wq, kr[...].astype(BF16), nt, preferred_element_type=F32),
                        -jnp.inf) for kr in k_refs]
    update(scores, [vr[...].astype(BF16) for vr in v_refs])

    @pl.when(j == pl.num_programs(1) - 1)
    def _():
        lam = _lam(lq1_ref, lk1_ref, lq2_ref, lk2_ref, lam_init)
        n = acc_ref[...] / l_ref[...]
        o_ref[...] = n[:qh] - lam * n[qh:]


def _decode_attention(q, k_new, v_new, cache_k, cache_v, page_table, lams, layer, lam_init):
    db, qh, width = q.shape
    n_pages = page_table.shape[1]
    pages = DECODE_PAGES
    page_rows = cache_k.shape[2]
    tok_spec = pl.BlockSpec((None, qh, width), lambda b, j, pt: (b, 0, 0))

    def page_spec(r):
        return pl.BlockSpec((None, None, page_rows, width),
                            lambda b, j, pt: (layer, pt[b * n_pages + j * pages + r], 0, 0))

    vec = pl.BlockSpec((1, HEAD_DIM), lambda b, j, pt: (0, 0))
    grid_spec = pltpu.PrefetchScalarGridSpec(
        num_scalar_prefetch=1,
        grid=(db, n_pages // pages),
        in_specs=([tok_spec] * 3 + [page_spec(r) for r in range(pages)]
                  + [page_spec(r) for r in range(pages)] + [vec] * 4),
        out_specs=tok_spec,
        scratch_shapes=[pltpu.VMEM((2 * qh, width), BF16),
                        pltpu.VMEM((2 * qh, 1), F32),
                        pltpu.VMEM((2 * qh, 1), F32),
                        pltpu.VMEM((2 * qh, width), F32),
                        pltpu.VMEM((page_rows, width), F32)],
    )
    return pl.pallas_call(
        functools.partial(_decode_kernel, n_new=qh // N_HEADS, pages=pages, lam_init=lam_init),
        grid_spec=grid_spec,
        out_shape=jax.ShapeDtypeStruct((db, qh, width), F32),
        compiler_params=pltpu.CompilerParams(dimension_semantics=("parallel", "arbitrary"),
                                             vmem_limit_bytes=VMEM_LIMIT),
        name="decode_attn",
    )(page_table.reshape(-1), q, k_new, v_new, *([cache_k] * pages), *([cache_v] * pages), *lams)


def kernel(x_prompt, x_sample, cache_k, cache_v, state_pool, page_table, norm_pre, norm_post,
           w_in, w_pool_mix, pool_scale, lambda_q1, lambda_k1, lambda_q2, lambda_k2, subln, w_out):
    batch, seq, d = x_prompt.shape
    db, n_new, _ = x_sample.shape
    depth = w_in.shape[0]
    n_phys = cache_k.shape[1]
    past_len = page_table.shape[1] * PAGE_SIZE
    assert seq % PROJ_ROWS == 0 and seq % ATTN_BLOCK == 0 and page_table.shape[1] % DECODE_PAGES == 0
    assert w_in.shape[2] == N_PROJ * ATTN_WIDTH and POOL_WIDTH == ATTN_WIDTH

    w_in_b = w_in.astype(BF16)
    w_out_b = w_out.astype(BF16)
    w_mix_b = w_pool_mix.astype(BF16)
    cache_k = cache_k.reshape(depth, n_phys, PAGE_SIZE * N_HEADS, V_DIM)
    cache_v = cache_v.reshape(depth, n_phys, PAGE_SIZE * N_HEADS, V_DIM)
    tables_p = _rope_tables(jnp.arange(seq))
    tables_s = _rope_tables(past_len + jnp.repeat(jnp.arange(n_new), db))
    xp = x_prompt.reshape(batch * seq, d)
    xs = x_sample.transpose(1, 0, 2).reshape(n_new * db, d)

    def to_batch_major(a):
        return (a.reshape(n_new, db, N_HEADS, V_DIM).transpose(1, 0, 2, 3)
                .reshape(db, n_new * N_HEADS, V_DIM))

    outs = [[] for _ in range(6)]
    for layer in range(depth):
        lam_init = _lam_init(layer)
        lams = [v[layer].reshape(1, HEAD_DIM) for v in (lambda_q1, lambda_k1, lambda_q2, lambda_k2)]
        u, gp, q, k, v, ga, kb, vb = _project(xp, norm_pre[layer], w_in_b[layer], tables_p, PROJ_ROWS)
        am = _prompt_attention(q, kb, vb, ga, subln[layer], lams, lam_init, batch, seq)
        xp = _prompt_finish(xp, u, gp, am, w_mix_b[layer], pool_scale[layer], w_out_b[layer],
                            norm_post[layer], seq, PROJ_ROWS)
        outs[0].append(k.reshape(batch, seq, N_HEADS, V_DIM))
        outs[1].append(v.reshape(batch, seq, N_HEADS, V_DIM))
        outs[2].append(u.reshape(batch, seq, POOL_WIDTH)[:, seq - POOL_BUF:])
        u, gp, q, k, v, ga, _, _ = _project(xs, norm_pre[layer], w_in_b[layer], tables_s, n_new * db)
        ao = _decode_attention(to_batch_major(q.astype(F32)), to_batch_major(k), to_batch_major(v),
                               cache_k, cache_v, page_table, lams, layer, lam_init)
        hist = jnp.concatenate([state_pool[layer].transpose(1, 0, 2),
                                u.reshape(n_new, db, POOL_WIDTH)], axis=0)
        ao = ao.reshape(db, n_new, ATTN_WIDTH).transpose(1, 0, 2).reshape(n_new * db, ATTN_WIDTH)
        xs = _sample_finish(xs, hist, gp, ao, ga,
                            subln[layer], w_mix_b[layer], pool_scale[layer], w_out_b[layer],
                            norm_post[layer], n_new, past_len, lam_init)
        outs[3].append(to_batch_major(k).reshape(db, n_new, N_HEADS, V_DIM))
        outs[4].append(to_batch_major(v).reshape(db, n_new, N_HEADS, V_DIM))
        outs[5].append(hist[n_new:].transpose(1, 0, 2))
    y_prompt = xp.reshape(batch, seq, d)
    y_sample = xs.reshape(n_new, db, d).transpose(1, 0, 2)
    return (y_prompt, y_sample) + tuple(jnp.stack(o) for o in outs)
```

```python
import functools
import math

import jax
import jax.numpy as jnp
from jax import lax
from jax.experimental import pallas as pl
from jax.experimental.pallas import tpu as pltpu

F32 = jnp.float32
BF16 = jnp.bfloat16

POOL_WINDOWS = (2, 4, 8, 16)
POOL_GROUP = 128
POOL_WIDTH = POOL_GROUP * len(POOL_WINDOWS)
POOL_BUF = max(POOL_WINDOWS) - 1
HEAD_DIM = 64
V_DIM = 2 * HEAD_DIM
N_HEADS = 4
ATTN_WIDTH = N_HEADS * V_DIM
ROT_DIM = HEAD_DIM // 4
ROPE_THETA = 500000.0
EPS = 1e-6
PAGE_SIZE = 128
N_PROJ = 6

LANES = 128
SUBLANES = 8
HALO = 2 * SUBLANES
VMEM_LIMIT = 56 * 1024 * 1024

PROJ_ROWS = 512
ATTN_BLOCK = PROJ_ROWS
ATTN_Q_SUB = 256
ATTN_K_SUB = 128
DECODE_PAGES = 16


def _lam_init(layer):
    return 0.8 - 0.6 * math.exp(-0.3 * layer)


def _lam(lq1_ref, lk1_ref, lq2_ref, lk2_ref, lam_init):
    a = jnp.sum(lq1_ref[...] * lk1_ref[...], axis=-1, keepdims=True)
    b = jnp.sum(lq2_ref[...] * lk2_ref[...], axis=-1, keepdims=True)
    return jnp.exp(a) - jnp.exp(b) + lam_init


def _silu(x):
    return x * jax.nn.sigmoid(x)


def _rms(x, g):
    return x * lax.rsqrt(jnp.mean(x * x, axis=-1, keepdims=True) + EPS) * g


def _rope_tables(pos):
    half = ROT_DIM // 2
    inv = ROPE_THETA ** (-(jnp.arange(half, dtype=F32) * 2.0) / ROT_DIM)
    ang = pos.astype(F32)[:, None] * inv[None, :]
    cos, sin = jnp.cos(ang), jnp.sin(ang)
    n = pos.shape[0]
    rest = HEAD_DIM - ROT_DIM
    zeros_h = jnp.zeros((n, half), F32)
    c = jnp.concatenate([cos, cos, jnp.ones((n, rest), F32)], axis=1)
    a = jnp.concatenate([-sin, zeros_h, jnp.zeros((n, rest), F32)], axis=1)
    b = jnp.concatenate([zeros_h, sin, jnp.zeros((n, rest), F32)], axis=1)
    rep = LANES // HEAD_DIM
    return tuple(jnp.tile(t, (1, rep)) for t in (c, a, b))


def _proj_kernel(x_ref, g_ref, w_ref, c_ref, a_ref, b_ref,
                 u_ref, gp_ref, k_ref, v_ref, ga_ref, *q_refs, for_prompt):
    xb = _rms(x_ref[...], g_ref[...]).astype(BF16)
    width = ATTN_WIDTH
    half = ROT_DIM // 2

    def mm(j):
        return jnp.dot(xb, w_ref[:, j * width:(j + 1) * width], preferred_element_type=F32)

    def rope(z, j):
        zc = z[:, j * LANES:(j + 1) * LANES]
        return (zc * c_ref[...] + pltpu.roll(zc, LANES - half, 1) * a_ref[...]
                + pltpu.roll(zc, half, 1) * b_ref[...])

    u_ref[...] = mm(0)
    gp_ref[...] = mm(1)
    q = mm(2)
    k = mm(3)
    v = mm(4)
    v_ref[...] = v
    for j in range(width // LANES):
        cols = slice(j * LANES, (j + 1) * LANES)
        qr = rope(q, j) * HEAD_DIM ** -0.5
        kr = rope(k, j)
        k_ref[:, cols] = kr
        if for_prompt:
            qt_ref, kb_ref, vt_ref = q_refs
            qt_ref[cols, :] = qr.T.astype(BF16)
            kb_ref[:, cols] = kr.astype(BF16)
            vt_ref[cols, :] = v[:, cols].T.astype(BF16)
        else:
            q_refs[0][:, cols] = qr
    ga_ref[...] = mm(5)


def _project(x, g_pre, w_in, tables, rows, for_prompt):
    t, d = x.shape
    n_blocks = t // rows
    n_tab = tables[0].shape[0] // rows
    width = ATTN_WIDTH
    row_spec = lambda w: pl.BlockSpec((rows, w), lambda i: (i, 0))
    tab_spec = pl.BlockSpec((rows, LANES), lambda i: (i % n_tab, 0))
    f32_out = jax.ShapeDtypeStruct((t, width), F32)
    out_specs = [row_spec(width)] * 5
    out_shape = [f32_out] * 5
    if for_prompt:
        t_spec = pl.BlockSpec((None, width, rows), lambda i: (i, 0, 0))
        t_out = jax.ShapeDtypeStruct((n_blocks, width, rows), BF16)
        out_specs += [t_spec, row_spec(width), t_spec]
        out_shape += [t_out, jax.ShapeDtypeStruct((t, width), BF16), t_out]
    else:
        out_specs += [row_spec(width)]
        out_shape += [f32_out]
    return pl.pallas_call(
        functools.partial(_proj_kernel, for_prompt=for_prompt),
        grid=(n_blocks,),
        in_specs=[row_spec(d), pl.BlockSpec((1, d), lambda i: (0, 0)),
                  pl.BlockSpec((d, N_PROJ * width), lambda i: (0, 0)),
                  tab_spec, tab_spec, tab_spec],
        out_specs=out_specs,
        out_shape=out_shape,
        compiler_params=pltpu.CompilerParams(dimension_semantics=("parallel",),
                                             vmem_limit_bytes=VMEM_LIMIT),
        name="proj",
    )(x, g_pre.reshape(1, d), w_in, *tables)


def _attn_kernel(qt_ref, k_ref, vt_ref, ga_ref, sub_ref, lq1_ref, lk1_ref, lq2_ref, lk2_ref,
                 o_ref, m_ref, l_ref, acc_ref, *, blk, lam_init):
    qb = pl.program_id(2)
    n_q = blk // ATTN_Q_SUB
    n_k = blk // ATTN_K_SUB
    m_ref[...] = jnp.full(m_ref.shape, -jnp.inf, F32)
    l_ref[...] = jnp.zeros(l_ref.shape, F32)
    acc_ref[...] = jnp.zeros(acc_ref.shape, F32)
    feat = lax.broadcasted_iota(jnp.int32, (V_DIM, ATTN_Q_SUB), 0)

    def unit(half, j, kb, kk, masked):
        idx = half * n_q + j
        qt = qt_ref[:, j * ATTN_Q_SUB:(j + 1) * ATTN_Q_SUB]
        keep = feat < HEAD_DIM if half == 0 else feat >= HEAD_DIM
        qt = jnp.where(keep, qt, jnp.zeros_like(qt))
        start = pl.multiple_of(kb * blk + kk * ATTN_K_SUB, ATTN_K_SUB)
        st = jnp.dot(k_ref[pl.ds(start, ATTN_K_SUB), :], qt, preferred_element_type=F32)
        if masked:
            key = kk * ATTN_K_SUB + lax.broadcasted_iota(jnp.int32, st.shape, 0)
            qi = j * ATTN_Q_SUB + lax.broadcasted_iota(jnp.int32, st.shape, 1)
            st = jnp.where(key <= qi, st, -jnp.inf)
        m_old = m_ref[idx]
        m_new = jnp.maximum(m_old, jnp.max(st, axis=0, keepdims=True))
        alpha = jnp.exp(m_old - m_new)
        pt = jnp.exp(st - m_new)
        l_ref[idx] = alpha * l_ref[idx] + jnp.sum(pt, axis=0, keepdims=True)
        vt = vt_ref[kb, :, kk * ATTN_K_SUB:(kk + 1) * ATTN_K_SUB]
        acc_ref[idx] = alpha * acc_ref[idx] + jnp.dot(vt, pt.astype(BF16),
                                                      preferred_element_type=F32)
        m_ref[idx] = m_new

    def full_block(kb, carry):
        for kk in range(n_k):
            for half in range(2):
                for j in range(n_q):
                    unit(half, j, kb, kk, False)
        return carry

    lax.fori_loop(0, qb, full_block, 0)
    for kk in range(n_k):
        for half in range(2):
            for j in range(n_q):
                first_key, last_key = kk * ATTN_K_SUB, (kk + 1) * ATTN_K_SUB - 1
                first_q, last_q = j * ATTN_Q_SUB, (j + 1) * ATTN_Q_SUB - 1
                if first_key <= last_q:
                    unit(half, j, qb, kk, last_key > first_q)

    lam = _lam(lq1_ref, lk1_ref, lq2_ref, lk2_ref, lam_init)
    for j in range(n_q):
        o = (acc_ref[j] / l_ref[j] - lam * (acc_ref[n_q + j] / l_ref[n_q + j])).T
        rows = slice(j * ATTN_Q_SUB, (j + 1) * ATTN_Q_SUB)
        y = _rms(o, sub_ref[...]) * (1.0 - lam_init)
        o_ref[rows, :] = (y * _silu(ga_ref[rows, :])).astype(BF16)


def _prompt_attention(qt, kb, vt, ga, subln, lams, lam_init, batch, seq):
    t = kb.shape[0]
    blk = ATTN_BLOCK
    nq = seq // blk
    n_tiles = 2 * (blk // ATTN_Q_SUB)
    row_spec = pl.BlockSpec((blk, V_DIM), lambda b, h, i: (b * nq + i, h))
    qt_spec = pl.BlockSpec((None, V_DIM, blk), lambda b, h, i: (b * nq + i, h, 0))
    k_spec = pl.BlockSpec((seq, V_DIM), lambda b, h, i: (b, h))
    vt_spec = pl.BlockSpec((nq, V_DIM, blk), lambda b, h, i: (b, h, 0))
    vec = lambda n: pl.BlockSpec((1, n), lambda b, h, i: (0, 0))
    return pl.pallas_call(
        functools.partial(_attn_kernel, blk=blk, lam_init=lam_init),
        grid=(batch, N_HEADS, nq),
        in_specs=[qt_spec, k_spec, vt_spec, row_spec, vec(V_DIM)] + [vec(HEAD_DIM)] * 4,
        out_specs=row_spec,
        out_shape=jax.ShapeDtypeStruct((t, ATTN_WIDTH), BF16),
        scratch_shapes=[pltpu.VMEM((n_tiles, 1, ATTN_Q_SUB), F32),
                        pltpu.VMEM((n_tiles, 1, ATTN_Q_SUB), F32),
                        pltpu.VMEM((n_tiles, V_DIM, ATTN_Q_SUB), F32)],
        compiler_params=pltpu.CompilerParams(
            dimension_semantics=("parallel", "parallel", "arbitrary"),
            vmem_limit_bytes=VMEM_LIMIT),
        name="prompt_attn",
    )(qt, kb, vt, ga, subln.reshape(1, V_DIM), *lams)


def _mix_out(x, pooled, gp_ref, attn_part, wmix_ref, ps_ref, wout_ref, gpost_ref):
    parts = []
    for g, pg in enumerate(pooled):
        cols = slice(g * POOL_GROUP, (g + 1) * POOL_GROUP)
        mixed = jnp.dot(pg.astype(BF16), wmix_ref[g], preferred_element_type=F32) * ps_ref[:, cols]
        parts.append((mixed * _silu(gp_ref[:, cols])).astype(BF16))
    mixed_all = jnp.concatenate(parts + [attn_part], axis=1)
    z = jnp.dot(mixed_all, wout_ref[...], preferred_element_type=F32)
    return x + _rms(z, gpost_ref[...])


def _finish_kernel(x_ref, u_ref, halo_ref, gp_ref, am_ref, wmix_ref, ps_ref, wout_ref, gpost_ref,
                   y_ref, buf_ref, *, rows, blocks_per_seq):
    blk_in_seq = pl.program_id(0) % blocks_per_seq
    buf_ref[0:HALO, :] = jnp.where(blk_in_seq == 0, 0.0, halo_ref[...])
    buf_ref[HALO:, :] = u_ref[...]
    pos = lax.broadcasted_iota(jnp.int32, (rows, 1), 0) + blk_in_seq * rows
    pooled = []
    for g, w in enumerate(POOL_WINDOWS):
        cols = slice(g * POOL_GROUP, (g + 1) * POOL_GROUP)
        tok = buf_ref[HALO:HALO + rows, cols]
        acc = tok
        for j in range(1, w):
            acc = acc + buf_ref[HALO - j:HALO - j + rows, cols]
        cnt = jnp.minimum(w, pos + 1).astype(F32)
        pooled.append(acc / cnt - tok)
    y_ref[...] = _mix_out(x_ref[...], pooled, gp_ref, am_ref[...], wmix_ref, ps_ref, wout_ref,
                          gpost_ref)


def _prompt_finish(x, u, gp, am, w_mix, pool_scale, w_out, g_post, seq, rows):
    t, d = x.shape
    blocks_per_seq = seq // rows
    halo_per_block = rows // HALO
    row_spec = lambda w: pl.BlockSpec((rows, w), lambda i: (i, 0))
    halo_spec = pl.BlockSpec((HALO, POOL_WIDTH),
                             lambda i: (jnp.maximum(i * halo_per_block - 1, 0), 0))
    const = lambda shape: pl.BlockSpec(shape, lambda i: (0,) * len(shape))
    return pl.pallas_call(
        functools.partial(_finish_kernel, rows=rows, blocks_per_seq=blocks_per_seq),
        grid=(t // rows,),
        in_specs=[row_spec(d), row_spec(POOL_WIDTH), halo_spec, row_spec(POOL_WIDTH),
                  row_spec(ATTN_WIDTH), const(w_mix.shape), const((1, POOL_WIDTH)),
                  const(w_out.shape), const((1, d))],
        out_specs=row_spec(d),
        out_shape=jax.ShapeDtypeStruct((t, d), F32),
        scratch_shapes=[pltpu.VMEM((HALO + rows, POOL_WIDTH), F32)],
        compiler_params=pltpu.CompilerParams(dimension_semantics=("parallel",),
                                             vmem_limit_bytes=VMEM_LIMIT),
        name="prompt_finish",
    )(x, u, u, gp, am, w_mix, pool_scale.reshape(1, POOL_WIDTH), w_out, g_post.reshape(1, d))


def _sample_finish_kernel(x_ref, r_ref, gp_ref, ao_ref, ga_ref, sub_ref, wmix_ref, ps_ref,
                          wout_ref, gpost_ref, y_ref, *, n_new, counts, lam_init):
    pooled = []
    for g, w in enumerate(POOL_WINDOWS):
        cols = slice(g * POOL_GROUP, (g + 1) * POOL_GROUP)
        per_step = []
        for i in range(n_new):
            tok = r_ref[POOL_BUF + i, :, cols]
            acc = tok
            for j in range(1, w):
                acc = acc + r_ref[POOL_BUF + i - j, :, cols]
            per_step.append(acc / counts[g][i] - tok)
        pooled.append(jnp.concatenate(per_step, axis=0))
    heads = []
    for h in range(N_HEADS):
        cols = slice(h * V_DIM, (h + 1) * V_DIM)
        y = _rms(ao_ref[:, cols], sub_ref[...]) * (1.0 - lam_init)
        heads.append((y * _silu(ga_ref[:, cols])).astype(BF16))
    attn_part = jnp.concatenate(heads, axis=1)
    y_ref[...] = _mix_out(x_ref[...], pooled, gp_ref, attn_part, wmix_ref, ps_ref, wout_ref,
                          gpost_ref)


def _sample_finish(x, rows_hist, gp, ao, ga, subln, w_mix, pool_scale, w_out, g_post,
                   n_new, past_len, lam_init):
    t, d = x.shape
    counts = tuple(tuple(float(min(w, past_len + i + 1)) for i in range(n_new))
                   for w in POOL_WINDOWS)
    return pl.pallas_call(
        functools.partial(_sample_finish_kernel, n_new=n_new, counts=counts, lam_init=lam_init),
        out_shape=jax.ShapeDtypeStruct((t, d), F32),
        compiler_params=pltpu.CompilerParams(vmem_limit_bytes=VMEM_LIMIT),
        name="sample_finish",
    )(x, rows_hist, gp, ao, ga, subln.reshape(1, V_DIM), w_mix,
      pool_scale.reshape(1, POOL_WIDTH), w_out, g_post.reshape(1, d))


def _decode_kernel(pt_ref, q_ref, kn_ref, vn_ref, *rest, n_new, pages, lam_init):
    k_refs = rest[:pages]
    v_refs = rest[pages:2 * pages]
    lq1_ref, lk1_ref, lq2_ref, lk2_ref, o_ref, wq_ref, m_ref, l_ref, acc_ref, pad_ref = rest[2 * pages:]
    del pt_ref
    j = pl.program_id(1)
    qh = n_new * N_HEADS
    page_rows = PAGE_SIZE * N_HEADS
    nt = (((1,), (1,)), ((), ()))
    row = lax.broadcasted_iota(jnp.int32, (2 * qh, page_rows), 0)
    col = lax.broadcasted_iota(jnp.int32, (2 * qh, page_rows), 1)
    same_head = (row % N_HEADS) == (col % N_HEADS)

    def update(state, scores, vals):
        m, l, acc = state
        m_new = m
        for s in scores:
            m_new = jnp.maximum(m_new, jnp.max(s, axis=-1, keepdims=True))
        alpha = jnp.exp(m - m_new)
        l = alpha * l
        acc = alpha * acc
        for s, v in zip(scores, vals):
            p = jnp.exp(s - m_new)
            l = l + jnp.sum(p, axis=-1, keepdims=True)
            acc = acc + jnp.dot(p.astype(BF16), v, preferred_element_type=F32)
        return m_new, l, acc

    def load_state():
        return m_ref[...], l_ref[...], acc_ref[...]

    def store_state(state):
        m_ref[...], l_ref[...], acc_ref[...] = state

    @pl.when(j == 0)
    def _():
        q = q_ref[...]
        lane = lax.broadcasted_iota(jnp.int32, q.shape, 1)
        wq_ref[...] = jnp.concatenate([jnp.where(lane < HEAD_DIM, q, 0.0),
                                       jnp.where(lane >= HEAD_DIM, q, 0.0)], axis=0).astype(BF16)
        pad_ref[...] = jnp.zeros(pad_ref.shape, F32)
        pad_ref[0:qh, :] = kn_ref[...]
        s = lax.dot_general(wq_ref[...], pad_ref[...].astype(BF16), nt, preferred_element_type=F32)
        causal = (col // N_HEADS) <= (row % qh) // N_HEADS
        s = jnp.where(same_head, jnp.where(causal, s, -jnp.inf), -jnp.inf)
        pad_ref[0:qh, :] = vn_ref[...]
        empty = (jnp.full(m_ref.shape, -jnp.inf, F32), jnp.zeros(l_ref.shape, F32),
                 jnp.zeros(acc_ref.shape, F32))
        store_state(update(empty, [s], [pad_ref[...].astype(BF16)]))

    wq = wq_ref[...]
    scores = [jnp.where(same_head,
                        lax.dot_general(wq, kr[...].astype(BF16), nt, preferred_element_type=F32),
                        -jnp.inf) for kr in k_refs]
    store_state(update(load_state(), scores, [vr[...].astype(BF16) for vr in v_refs]))

    @pl.when(j == pl.num_programs(1) - 1)
    def _():
        lam = _lam(lq1_ref, lk1_ref, lq2_ref, lk2_ref, lam_init)
        n = acc_ref[...] / l_ref[...]
        o_ref[...] = n[:qh] - lam * n[qh:]


def _decode_attention(q, k_new, v_new, cache_k, cache_v, page_table, lams, layer, lam_init):
    db, qh, width = q.shape
    n_pages = page_table.shape[1]
    pages = DECODE_PAGES
    page_rows = cache_k.shape[2]
    tok_spec = pl.BlockSpec((None, qh, width), lambda b, j, pt: (b, 0, 0))

    def page_spec(r):
        return pl.BlockSpec((None, None, page_rows, width),
                            lambda b, j, pt: (layer, pt[b * n_pages + j * pages + r], 0, 0))

    vec = pl.BlockSpec((1, HEAD_DIM), lambda b, j, pt: (0, 0))
    grid_spec = pltpu.PrefetchScalarGridSpec(
        num_scalar_prefetch=1,
        grid=(db, n_pages // pages),
        in_specs=([tok_spec] * 3 + [page_spec(r) for r in range(pages)]
                  + [page_spec(r) for r in range(pages)] + [vec] * 4),
        out_specs=tok_spec,
        scratch_shapes=[pltpu.VMEM((2 * qh, width), BF16),
                        pltpu.VMEM((2 * qh, 1), F32),
                        pltpu.VMEM((2 * qh, 1), F32),
                        pltpu.VMEM((2 * qh, width), F32),
                        pltpu.VMEM((page_rows, width), F32)],
    )
    return pl.pallas_call(
        functools.partial(_decode_kernel, n_new=qh // N_HEADS, pages=pages, lam_init=lam_init),
        grid_spec=grid_spec,
        out_shape=jax.ShapeDtypeStruct((db, qh, width), F32),
        compiler_params=pltpu.CompilerParams(dimension_semantics=("parallel", "arbitrary"),
                                             vmem_limit_bytes=VMEM_LIMIT),
        name="decode_attn",
    )(page_table.reshape(-1), q, k_new, v_new, *([cache_k] * pages), *([cache_v] * pages), *lams)


def kernel(x_prompt, x_sample, cache_k, cache_v, state_pool, page_table, norm_pre, norm_post,
           w_in, w_pool_mix, pool_scale, lambda_q1, lambda_k1, lambda_q2, lambda_k2, subln, w_out):
    batch, seq, d = x_prompt.shape
    db, n_new, _ = x_sample.shape
    depth = w_in.shape[0]
    n_phys = cache_k.shape[1]
    past_len = page_table.shape[1] * PAGE_SIZE
    assert seq % PROJ_ROWS == 0 and seq % ATTN_BLOCK == 0 and page_table.shape[1] % DECODE_PAGES == 0
    assert w_in.shape[2] == N_PROJ * ATTN_WIDTH and POOL_WIDTH == ATTN_WIDTH

    w_in_b = w_in.astype(BF16)
    w_out_b = w_out.astype(BF16)
    w_mix_b = w_pool_mix.astype(BF16)
    cache_k = cache_k.reshape(depth, n_phys, PAGE_SIZE * N_HEADS, V_DIM)
    cache_v = cache_v.reshape(depth, n_phys, PAGE_SIZE * N_HEADS, V_DIM)
    tables_p = _rope_tables(jnp.arange(seq))
    tables_s = _rope_tables(past_len + jnp.repeat(jnp.arange(n_new), db))
    xp = x_prompt.reshape(batch * seq, d)
    xs = x_sample.transpose(1, 0, 2).reshape(n_new * db, d)

    def to_batch_major(a):
        return (a.reshape(n_new, db, N_HEADS, V_DIM).transpose(1, 0, 2, 3)
                .reshape(db, n_new * N_HEADS, V_DIM))

    outs = [[] for _ in range(6)]
    for layer in range(depth):
        lam_init = _lam_init(layer)
        lams = [v[layer].reshape(1, HEAD_DIM) for v in (lambda_q1, lambda_k1, lambda_q2, lambda_k2)]
        u, gp, k, v, ga, qt, kb, vt = _project(xp, norm_pre[layer], w_in_b[layer], tables_p,
                                               PROJ_ROWS, True)
        am = _prompt_attention(qt, kb, vt, ga, subln[layer], lams, lam_init, batch, seq)
        xp = _prompt_finish(xp, u, gp, am, w_mix_b[layer], pool_scale[layer], w_out_b[layer],
                            norm_post[layer], seq, PROJ_ROWS)
        outs[0].append(k.reshape(batch, seq, N_HEADS, V_DIM))
        outs[1].append(v.reshape(batch, seq, N_HEADS, V_DIM))
        outs[2].append(u.reshape(batch, seq, POOL_WIDTH)[:, seq - POOL_BUF:])
        u, gp, k, v, ga, q = _project(xs, norm_pre[layer], w_in_b[layer], tables_s, n_new * db, False)
        ao = _decode_attention(to_batch_major(q), to_batch_major(k), to_batch_major(v),
                               cache_k, cache_v, page_table, lams, layer, lam_init)
        hist = jnp.concatenate([state_pool[layer].transpose(1, 0, 2),
                                u.reshape(n_new, db, POOL_WIDTH)], axis=0)
        ao = ao.reshape(db, n_new, ATTN_WIDTH).transpose(1, 0, 2).reshape(n_new * db, ATTN_WIDTH)
        xs = _sample_finish(xs, hist, gp, ao, ga,
                            subln[layer], w_mix_b[layer], pool_scale[layer], w_out_b[layer],
                            norm_post[layer], n_new, past_len, lam_init)
        outs[3].append(to_batch_major(k).reshape(db, n_new, N_HEADS, V_DIM))
        outs[4].append(to_batch_major(v).reshape(db, n_new, N_HEADS, V_DIM))
        outs[5].append(hist[n_new:].transpose(1, 0, 2))
    y_prompt = xp.reshape(batch, seq, d)
    y_sample = xs.reshape(n_new, db, d).transpose(1, 0, 2)
    return (y_prompt, y_sample) + tuple(jnp.stack(o) for o in outs)
```

```python
import functools
import math

import jax
import jax.numpy as jnp
from jax import lax
from jax.experimental import pallas as pl
from jax.experimental.pallas import tpu as pltpu

F32 = jnp.float32
BF16 = jnp.bfloat16

POOL_WINDOWS = (2, 4, 8, 16)
POOL_GROUP = 128
POOL_WIDTH = POOL_GROUP * len(POOL_WINDOWS)
POOL_BUF = max(POOL_WINDOWS) - 1
HEAD_DIM = 64
V_DIM = 2 * HEAD_DIM
N_HEADS = 4
ATTN_WIDTH = N_HEADS * V_DIM
ROT_DIM = HEAD_DIM // 4
ROPE_THETA = 500000.0
EPS = 1e-6
PAGE_SIZE = 128
N_PROJ = 6

LANES = 128
SUBLANES = 8
HALO = 2 * SUBLANES
VMEM_LIMIT = 56 * 1024 * 1024

PROJ_ROWS = 512
ATTN_BLOCK = PROJ_ROWS
ATTN_Q_SUB = 256
ATTN_K_SUB = 128
DECODE_PAGES = 16
DECODE_SLOTS = 3


def _lam_init(layer):
    return 0.8 - 0.6 * math.exp(-0.3 * layer)


def _lam(lq1_ref, lk1_ref, lq2_ref, lk2_ref, lam_init):
    a = jnp.sum(lq1_ref[...] * lk1_ref[...], axis=-1, keepdims=True)
    b = jnp.sum(lq2_ref[...] * lk2_ref[...], axis=-1, keepdims=True)
    return jnp.exp(a) - jnp.exp(b) + lam_init


def _silu(x):
    return x * jax.nn.sigmoid(x)


def _rms(x, g):
    return x * lax.rsqrt(jnp.mean(x * x, axis=-1, keepdims=True) + EPS) * g


def _rope_tables(pos):
    half = ROT_DIM // 2
    inv = ROPE_THETA ** (-(jnp.arange(half, dtype=F32) * 2.0) / ROT_DIM)
    ang = pos.astype(F32)[:, None] * inv[None, :]
    cos, sin = jnp.cos(ang), jnp.sin(ang)
    n = pos.shape[0]
    rest = HEAD_DIM - ROT_DIM
    zeros_h = jnp.zeros((n, half), F32)
    c = jnp.concatenate([cos, cos, jnp.ones((n, rest), F32)], axis=1)
    a = jnp.concatenate([-sin, zeros_h, jnp.zeros((n, rest), F32)], axis=1)
    b = jnp.concatenate([zeros_h, sin, jnp.zeros((n, rest), F32)], axis=1)
    rep = LANES // HEAD_DIM
    return tuple(jnp.tile(t, (1, rep)) for t in (c, a, b))


def _proj_kernel(x_ref, g_ref, w_ref, c_ref, a_ref, b_ref,
                 u_ref, gp_ref, k_ref, v_ref, ga_ref, *q_refs, for_prompt):
    xb = _rms(x_ref[...], g_ref[...]).astype(BF16)
    width = ATTN_WIDTH
    half = ROT_DIM // 2

    def mm(j):
        return jnp.dot(xb, w_ref[:, j * width:(j + 1) * width], preferred_element_type=F32)

    def rope(z, j):
        zc = z[:, j * LANES:(j + 1) * LANES]
        return (zc * c_ref[...] + pltpu.roll(zc, LANES - half, 1) * a_ref[...]
                + pltpu.roll(zc, half, 1) * b_ref[...])

    u_ref[...] = mm(0)
    gp_ref[...] = mm(1)
    q = mm(2)
    k = mm(3)
    v = mm(4)
    v_ref[...] = v
    for j in range(width // LANES):
        cols = slice(j * LANES, (j + 1) * LANES)
        qr = rope(q, j) * HEAD_DIM ** -0.5
        kr = rope(k, j)
        k_ref[:, cols] = kr
        if for_prompt:
            qt_ref, kb_ref, vt_ref = q_refs
            qt_ref[cols, :] = qr.T.astype(BF16)
            kb_ref[:, cols] = kr.astype(BF16)
            vt_ref[cols, :] = v[:, cols].T.astype(BF16)
        else:
            q_refs[0][:, cols] = qr
    ga_ref[...] = mm(5)


def _project(x, g_pre, w_in, tables, rows, for_prompt):
    t, d = x.shape
    n_blocks = t // rows
    n_tab = tables[0].shape[0] // rows
    width = ATTN_WIDTH
    row_spec = lambda w: pl.BlockSpec((rows, w), lambda i: (i, 0))
    tab_spec = pl.BlockSpec((rows, LANES), lambda i: (i % n_tab, 0))
    f32_out = jax.ShapeDtypeStruct((t, width), F32)
    out_specs = [row_spec(width)] * 5
    out_shape = [f32_out] * 5
    if for_prompt:
        t_spec = pl.BlockSpec((None, width, rows), lambda i: (i, 0, 0))
        t_out = jax.ShapeDtypeStruct((n_blocks, width, rows), BF16)
        out_specs += [t_spec, row_spec(width), t_spec]
        out_shape += [t_out, jax.ShapeDtypeStruct((t, width), BF16), t_out]
    else:
        out_specs += [row_spec(width)]
        out_shape += [f32_out]
    return pl.pallas_call(
        functools.partial(_proj_kernel, for_prompt=for_prompt),
        grid=(n_blocks,),
        in_specs=[row_spec(d), pl.BlockSpec((1, d), lambda i: (0, 0)),
                  pl.BlockSpec((d, N_PROJ * width), lambda i: (0, 0)),
                  tab_spec, tab_spec, tab_spec],
        out_specs=out_specs,
        out_shape=out_shape,
        compiler_params=pltpu.CompilerParams(dimension_semantics=("parallel",),
                                             vmem_limit_bytes=VMEM_LIMIT),
        name="proj",
    )(x, g_pre.reshape(1, d), w_in, *tables)


def _attn_kernel(qt_ref, k_ref, vt_ref, ga_ref, sub_ref, lq1_ref, lk1_ref, lq2_ref, lk2_ref,
                 o_ref, m_ref, l_ref, acc_ref, *, blk, lam_init):
    qb = pl.program_id(2)
    n_q = blk // ATTN_Q_SUB
    n_k = blk // ATTN_K_SUB
    m_ref[...] = jnp.full(m_ref.shape, -jnp.inf, F32)
    l_ref[...] = jnp.zeros(l_ref.shape, F32)
    acc_ref[...] = jnp.zeros(acc_ref.shape, F32)
    feat = lax.broadcasted_iota(jnp.int32, (V_DIM, ATTN_Q_SUB), 0)

    def unit(half, j, kb, kk, masked):
        idx = half * n_q + j
        qt = qt_ref[:, j * ATTN_Q_SUB:(j + 1) * ATTN_Q_SUB]
        keep = feat < HEAD_DIM if half == 0 else feat >= HEAD_DIM
        qt = jnp.where(keep, qt, jnp.zeros_like(qt))
        start = pl.multiple_of(kb * blk + kk * ATTN_K_SUB, ATTN_K_SUB)
        st = jnp.dot(k_ref[pl.ds(start, ATTN_K_SUB), :], qt, preferred_element_type=F32)
        if masked:
            key = kk * ATTN_K_SUB + lax.broadcasted_iota(jnp.int32, st.shape, 0)
            qi = j * ATTN_Q_SUB + lax.broadcasted_iota(jnp.int32, st.shape, 1)
            st = jnp.where(key <= qi, st, -jnp.inf)
        m_old = m_ref[idx]
        m_new = jnp.maximum(m_old, jnp.max(st, axis=0, keepdims=True))
        alpha = jnp.exp(m_old - m_new)
        pt = jnp.exp(st - m_new)
        l_ref[idx] = alpha * l_ref[idx] + jnp.sum(pt, axis=0, keepdims=True)
        vt = vt_ref[kb, :, kk * ATTN_K_SUB:(kk + 1) * ATTN_K_SUB]
        acc_ref[idx] = alpha * acc_ref[idx] + jnp.dot(vt, pt.astype(BF16),
                                                      preferred_element_type=F32)
        m_ref[idx] = m_new

    def full_block(kb, carry):
        for kk in range(n_k):
            for half in range(2):
                for j in range(n_q):
                    unit(half, j, kb, kk, False)
        return carry

    lax.fori_loop(0, qb, full_block, 0)
    for kk in range(n_k):
        for half in range(2):
            for j in range(n_q):
                first_key, last_key = kk * ATTN_K_SUB, (kk + 1) * ATTN_K_SUB - 1
                first_q, last_q = j * ATTN_Q_SUB, (j + 1) * ATTN_Q_SUB - 1
                if first_key <= last_q:
                    unit(half, j, qb, kk, last_key > first_q)

    lam = _lam(lq1_ref, lk1_ref, lq2_ref, lk2_ref, lam_init)
    for j in range(n_q):
        o = (acc_ref[j] / l_ref[j] - lam * (acc_ref[n_q + j] / l_ref[n_q + j])).T
        rows = slice(j * ATTN_Q_SUB, (j + 1) * ATTN_Q_SUB)
        y = _rms(o, sub_ref[...]) * (1.0 - lam_init)
        o_ref[rows, :] = (y * _silu(ga_ref[rows, :])).astype(BF16)


def _prompt_attention(qt, kb, vt, ga, subln, lams, lam_init, batch, seq):
    t = kb.shape[0]
    blk = ATTN_BLOCK
    nq = seq // blk
    n_tiles = 2 * (blk // ATTN_Q_SUB)
    row_spec = pl.BlockSpec((blk, V_DIM), lambda b, h, i: (b * nq + i, h))
    qt_spec = pl.BlockSpec((None, V_DIM, blk), lambda b, h, i: (b * nq + i, h, 0))
    k_spec = pl.BlockSpec((seq, V_DIM), lambda b, h, i: (b, h))
    vt_spec = pl.BlockSpec((nq, V_DIM, blk), lambda b, h, i: (b, h, 0))
    vec = lambda n: pl.BlockSpec((1, n), lambda b, h, i: (0, 0))
    return pl.pallas_call(
        functools.partial(_attn_kernel, blk=blk, lam_init=lam_init),
        grid=(batch, N_HEADS, nq),
        in_specs=[qt_spec, k_spec, vt_spec, row_spec, vec(V_DIM)] + [vec(HEAD_DIM)] * 4,
        out_specs=row_spec,
        out_shape=jax.ShapeDtypeStruct((t, ATTN_WIDTH), BF16),
        scratch_shapes=[pltpu.VMEM((n_tiles, 1, ATTN_Q_SUB), F32),
                        pltpu.VMEM((n_tiles, 1, ATTN_Q_SUB), F32),
                        pltpu.VMEM((n_tiles, V_DIM, ATTN_Q_SUB), F32)],
        compiler_params=pltpu.CompilerParams(
            dimension_semantics=("parallel", "parallel", "arbitrary"),
            vmem_limit_bytes=VMEM_LIMIT),
        name="prompt_attn",
    )(qt, kb, vt, ga, subln.reshape(1, V_DIM), *lams)


def _mix_out(x, pooled, gp_ref, attn_part, wmix_ref, ps_ref, wout_ref, gpost_ref):
    parts = []
    for g, pg in enumerate(pooled):
        cols = slice(g * POOL_GROUP, (g + 1) * POOL_GROUP)
        mixed = jnp.dot(pg.astype(BF16), wmix_ref[g], preferred_element_type=F32) * ps_ref[:, cols]
        parts.append((mixed * _silu(gp_ref[:, cols])).astype(BF16))
    mixed_all = jnp.concatenate(parts + [attn_part], axis=1)
    z = jnp.dot(mixed_all, wout_ref[...], preferred_element_type=F32)
    return x + _rms(z, gpost_ref[...])


def _finish_kernel(x_ref, u_ref, halo_ref, gp_ref, am_ref, wmix_ref, ps_ref, wout_ref, gpost_ref,
                   y_ref, buf_ref, *, rows, blocks_per_seq):
    blk_in_seq = pl.program_id(0) % blocks_per_seq
    buf_ref[0:HALO, :] = jnp.where(blk_in_seq == 0, 0.0, halo_ref[...])
    buf_ref[HALO:, :] = u_ref[...]
    pos = lax.broadcasted_iota(jnp.int32, (rows, 1), 0) + blk_in_seq * rows
    pooled = []
    for g, w in enumerate(POOL_WINDOWS):
        cols = slice(g * POOL_GROUP, (g + 1) * POOL_GROUP)
        tok = buf_ref[HALO:HALO + rows, cols]
        acc = tok
        for j in range(1, w):
            acc = acc + buf_ref[HALO - j:HALO - j + rows, cols]
        cnt = jnp.minimum(w, pos + 1).astype(F32)
        pooled.append(acc / cnt - tok)
    y_ref[...] = _mix_out(x_ref[...], pooled, gp_ref, am_ref[...], wmix_ref, ps_ref, wout_ref,
                          gpost_ref)


def _prompt_finish(x, u, gp, am, w_mix, pool_scale, w_out, g_post, seq, rows):
    t, d = x.shape
    blocks_per_seq = seq // rows
    halo_per_block = rows // HALO
    row_spec = lambda w: pl.BlockSpec((rows, w), lambda i: (i, 0))
    halo_spec = pl.BlockSpec((HALO, POOL_WIDTH),
                             lambda i: (jnp.maximum(i * halo_per_block - 1, 0), 0))
    const = lambda shape: pl.BlockSpec(shape, lambda i: (0,) * len(shape))
    return pl.pallas_call(
        functools.partial(_finish_kernel, rows=rows, blocks_per_seq=blocks_per_seq),
        grid=(t // rows,),
        in_specs=[row_spec(d), row_spec(POOL_WIDTH), halo_spec, row_spec(POOL_WIDTH),
                  row_spec(ATTN_WIDTH), const(w_mix.shape), const((1, POOL_WIDTH)),
                  const(w_out.shape), const((1, d))],
        out_specs=row_spec(d),
        out_shape=jax.ShapeDtypeStruct((t, d), F32),
        scratch_shapes=[pltpu.VMEM((HALO + rows, POOL_WIDTH), F32)],
        compiler_params=pltpu.CompilerParams(dimension_semantics=("parallel",),
                                             vmem_limit_bytes=VMEM_LIMIT),
        name="prompt_finish",
    )(x, u, u, gp, am, w_mix, pool_scale.reshape(1, POOL_WIDTH), w_out, g_post.reshape(1, d))


def _sample_finish_kernel(x_ref, r_ref, gp_ref, ao_ref, ga_ref, sub_ref, wmix_ref, ps_ref,
                          wout_ref, gpost_ref, y_ref, *, n_new, counts, lam_init):
    pooled = []
    for g, w in enumerate(POOL_WINDOWS):
        cols = slice(g * POOL_GROUP, (g + 1) * POOL_GROUP)
        per_step = []
        for i in range(n_new):
            tok = r_ref[POOL_BUF + i, :, cols]
            acc = tok
            for j in range(1, w):
                acc = acc + r_ref[POOL_BUF + i - j, :, cols]
            per_step.append(acc / counts[g][i] - tok)
        pooled.append(jnp.concatenate(per_step, axis=0))
    heads = []
    for h in range(N_HEADS):
        cols = slice(h * V_DIM, (h + 1) * V_DIM)
        y = _rms(ao_ref[:, cols], sub_ref[...]) * (1.0 - lam_init)
        heads.append((y * _silu(ga_ref[:, cols])).astype(BF16))
    attn_part = jnp.concatenate(heads, axis=1)
    y_ref[...] = _mix_out(x_ref[...], pooled, gp_ref, attn_part, wmix_ref, ps_ref, wout_ref,
                          gpost_ref)


def _sample_finish(x, rows_hist, gp, ao, ga, subln, w_mix, pool_scale, w_out, g_post,
                   n_new, past_len, lam_init):
    t, d = x.shape
    counts = tuple(tuple(float(min(w, past_len + i + 1)) for i in range(n_new))
                   for w in POOL_WINDOWS)
    return pl.pallas_call(
        functools.partial(_sample_finish_kernel, n_new=n_new, counts=counts, lam_init=lam_init),
        out_shape=jax.ShapeDtypeStruct((t, d), F32),
        compiler_params=pltpu.CompilerParams(vmem_limit_bytes=VMEM_LIMIT),
        name="sample_finish",
    )(x, rows_hist, gp, ao, ga, subln.reshape(1, V_DIM), w_mix,
      pool_scale.reshape(1, POOL_WIDTH), w_out, g_post.reshape(1, d))


def _decode_kernel(pt_ref, q_ref, kn_ref, vn_ref, ck_hbm, cv_hbm, lq1_ref, lk1_ref, lq2_ref, lk2_ref,
                   o_ref, kbuf, vbuf, sem, wq_ref, m_ref, l_ref, acc_ref, pad_ref,
                   *, n_new, pages, layer, lam_init):
    j = pl.program_id(1)
    n_steps = pl.num_programs(1)
    t = pl.program_id(0) * n_steps + j
    total = pl.num_programs(0) * n_steps

    def copies(step, slot):
        out = []
        for r in range(pages):
            page = pt_ref[step * pages + r]
            out.append(pltpu.make_async_copy(ck_hbm.at[layer, page], kbuf.at[slot, r], sem.at[0, slot]))
            out.append(pltpu.make_async_copy(cv_hbm.at[layer, page], vbuf.at[slot, r], sem.at[1, slot]))
        return out

    @pl.when(t == 0)
    def _():
        for step in range(DECODE_SLOTS - 1):
            for c in copies(step, step):
                c.start()

    ahead = t + (DECODE_SLOTS - 1)

    @pl.when(ahead < total)
    def _():
        for c in copies(ahead, ahead % DECODE_SLOTS):
            c.start()

    slot = t % DECODE_SLOTS
    for c in copies(t, slot):
        c.wait()
    k_refs = [kbuf.at[slot, r] for r in range(pages)]
    v_refs = [vbuf.at[slot, r] for r in range(pages)]
    qh = n_new * N_HEADS
    page_rows = PAGE_SIZE * N_HEADS
    nt = (((1,), (1,)), ((), ()))
    row = lax.broadcasted_iota(jnp.int32, (2 * qh, page_rows), 0)
    col = lax.broadcasted_iota(jnp.int32, (2 * qh, page_rows), 1)
    same_head = (row % N_HEADS) == (col % N_HEADS)

    def update(state, scores, vals):
        m, l, acc = state
        m_new = m
        for s in scores:
            m_new = jnp.maximum(m_new, jnp.max(s, axis=-1, keepdims=True))
        alpha = jnp.exp(m - m_new)
        l = alpha * l
        acc = alpha * acc
        for s, v in zip(scores, vals):
            p = jnp.exp(s - m_new)
            l = l + jnp.sum(p, axis=-1, keepdims=True)
            acc = acc + jnp.dot(p.astype(BF16), v, preferred_element_type=F32)
        return m_new, l, acc

    def load_state():
        return m_ref[...], l_ref[...], acc_ref[...]

    def store_state(state):
        m_ref[...], l_ref[...], acc_ref[...] = state

    @pl.when(j == 0)
    def _():
        q = q_ref[...]
        lane = lax.broadcasted_iota(jnp.int32, q.shape, 1)
        wq_ref[...] = jnp.concatenate([jnp.where(lane < HEAD_DIM, q, 0.0),
                                       jnp.where(lane >= HEAD_DIM, q, 0.0)], axis=0).astype(BF16)
        pad_ref[...] = jnp.zeros(pad_ref.shape, F32)
        pad_ref[0:qh, :] = kn_ref[...]
        s = lax.dot_general(wq_ref[...], pad_ref[...].astype(BF16), nt, preferred_element_type=F32)
        causal = (col // N_HEADS) <= (row % qh) // N_HEADS
        s = jnp.where(same_head, jnp.where(causal, s, -jnp.inf), -jnp.inf)
        pad_ref[0:qh, :] = vn_ref[...]
        empty = (jnp.full(m_ref.shape, -jnp.inf, F32), jnp.zeros(l_ref.shape, F32),
                 jnp.zeros(acc_ref.shape, F32))
        store_state(update(empty, [s], [pad_ref[...].astype(BF16)]))

    wq = wq_ref[...]
    scores = [jnp.where(same_head,
                        lax.dot_general(wq, kr[...].astype(BF16), nt, preferred_element_type=F32),
                        -jnp.inf) for kr in k_refs]
    store_state(update(load_state(), scores, [vr[...].astype(BF16) for vr in v_refs]))

    @pl.when(j == pl.num_programs(1) - 1)
    def _():
        lam = _lam(lq1_ref, lk1_ref, lq2_ref, lk2_ref, lam_init)
        n = acc_ref[...] / l_ref[...]
        o_ref[...] = n[:qh] - lam * n[qh:]


def _decode_attention(q, k_new, v_new, cache_k, cache_v, page_table, lams, layer, lam_init):
    db, qh, width = q.shape
    n_pages = page_table.shape[1]
    pages = DECODE_PAGES
    page_rows = cache_k.shape[2]
    n_steps = n_pages // pages
    assert db * n_steps >= DECODE_SLOTS - 1
    tok_spec = pl.BlockSpec((None, qh, width), lambda b, j, pt: (b, 0, 0))
    hbm_spec = pl.BlockSpec(memory_space=pl.ANY)
    vec = pl.BlockSpec((1, HEAD_DIM), lambda b, j, pt: (0, 0))
    ring = pltpu.VMEM((DECODE_SLOTS, pages, page_rows, width), F32)
    grid_spec = pltpu.PrefetchScalarGridSpec(
        num_scalar_prefetch=1,
        grid=(db, n_steps),
        in_specs=[tok_spec] * 3 + [hbm_spec] * 2 + [vec] * 4,
        out_specs=tok_spec,
        scratch_shapes=[ring, ring, pltpu.SemaphoreType.DMA((2, DECODE_SLOTS)),
                        pltpu.VMEM((2 * qh, width), BF16),
                        pltpu.VMEM((2 * qh, 1), F32),
                        pltpu.VMEM((2 * qh, 1), F32),
                        pltpu.VMEM((2 * qh, width), F32),
                        pltpu.VMEM((page_rows, width), F32)],
    )
    return pl.pallas_call(
        functools.partial(_decode_kernel, n_new=qh // N_HEADS, pages=pages, layer=layer,
                          lam_init=lam_init),
        grid_spec=grid_spec,
        out_shape=jax.ShapeDtypeStruct((db, qh, width), F32),
        compiler_params=pltpu.CompilerParams(dimension_semantics=("arbitrary", "arbitrary"),
                                             vmem_limit_bytes=VMEM_LIMIT),
        name="decode_attn",
    )(page_table.reshape(-1), q, k_new, v_new, cache_k, cache_v, *lams)


def kernel(x_prompt, x_sample, cache_k, cache_v, state_pool, page_table, norm_pre, norm_post,
           w_in, w_pool_mix, pool_scale, lambda_q1, lambda_k1, lambda_q2, lambda_k2, subln, w_out):
    batch, seq, d = x_prompt.shape
    db, n_new, _ = x_sample.shape
    depth = w_in.shape[0]
    n_phys = cache_k.shape[1]
    past_len = page_table.shape[1] * PAGE_SIZE
    assert seq % PROJ_ROWS == 0 and seq % ATTN_BLOCK == 0 and page_table.shape[1] % DECODE_PAGES == 0
    assert w_in.shape[2] == N_PROJ * ATTN_WIDTH and POOL_WIDTH == ATTN_WIDTH

    w_in_b = w_in.astype(BF16)
    w_out_b = w_out.astype(BF16)
    w_mix_b = w_pool_mix.astype(BF16)
    cache_k = cache_k.reshape(depth, n_phys, PAGE_SIZE * N_HEADS, V_DIM)
    cache_v = cache_v.reshape(depth, n_phys, PAGE_SIZE * N_HEADS, V_DIM)
    tables_p = _rope_tables(jnp.arange(seq))
    tables_s = _rope_tables(past_len + jnp.repeat(jnp.arange(n_new), db))
    xp = x_prompt.reshape(batch * seq, d)
    xs = x_sample.transpose(1, 0, 2).reshape(n_new * db, d)

    def to_batch_major(a):
        return (a.reshape(n_new, db, N_HEADS, V_DIM).transpose(1, 0, 2, 3)
                .reshape(db, n_new * N_HEADS, V_DIM))

    outs = [[] for _ in range(6)]
    for layer in range(depth):
        lam_init = _lam_init(layer)
        lams = [v[layer].reshape(1, HEAD_DIM) for v in (lambda_q1, lambda_k1, lambda_q2, lambda_k2)]
        u, gp, k, v, ga, qt, kb, vt = _project(xp, norm_pre[layer], w_in_b[layer], tables_p,
                                               PROJ_ROWS, True)
        am = _prompt_attention(qt, kb, vt, ga, subln[layer], lams, lam_init, batch, seq)
        xp = _prompt_finish(xp, u, gp, am, w_mix_b[layer], pool_scale[layer], w_out_b[layer],
                            norm_post[layer], seq, PROJ_ROWS)
        outs[0].append(k.reshape(batch, seq, N_HEADS, V_DIM))
        outs[1].append(v.reshape(batch, seq, N_HEADS, V_DIM))
        outs[2].append(u.reshape(batch, seq, POOL_WIDTH)[:, seq - POOL_BUF:])
        u, gp, k, v, ga, q = _project(xs, norm_pre[layer], w_in_b[layer], tables_s, n_new * db, False)
        ao = _decode_attention(to_batch_major(q), to_batch_major(k), to_batch_major(v),
                               cache_k, cache_v, page_table, lams, layer, lam_init)
        hist = jnp.concatenate([state_pool[layer].transpose(1, 0, 2),
                                u.reshape(n_new, db, POOL_WIDTH)], axis=0)
        ao = ao.reshape(db, n_new, ATTN_WIDTH).transpose(1, 0, 2).reshape(n_new * db, ATTN_WIDTH)
        xs = _sample_finish(xs, hist, gp, ao, ga,
                            subln[layer], w_mix_b[layer], pool_scale[layer], w_out_b[layer],
                            norm_post[layer], n_new, past_len, lam_init)
        outs[3].append(to_batch_major(k).reshape(db, n_new, N_HEADS, V_DIM))
        outs[4].append(to_batch_major(v).reshape(db, n_new, N_HEADS, V_DIM))
        outs[5].append(hist[n_new:].transpose(1, 0, 2))
    y_prompt = xp.reshape(batch, seq, d)
    y_sample = xs.reshape(n_new, db, d).transpose(1, 0, 2)
    return (y_prompt, y_sample) + tuple(jnp.stack(o) for o in outs)
```

```python
import functools
import math

import jax
import jax.numpy as jnp
from jax import lax
from jax.experimental import pallas as pl
from jax.experimental.pallas import tpu as pltpu

F32 = jnp.float32
BF16 = jnp.bfloat16

POOL_WINDOWS = (2, 4, 8, 16)
POOL_GROUP = 128
POOL_WIDTH = POOL_GROUP * len(POOL_WINDOWS)
POOL_BUF = max(POOL_WINDOWS) - 1
HEAD_DIM = 64
V_DIM = 2 * HEAD_DIM
N_HEADS = 4
ATTN_WIDTH = N_HEADS * V_DIM
ROT_DIM = HEAD_DIM // 4
ROPE_THETA = 500000.0
EPS = 1e-6
PAGE_SIZE = 128
N_PROJ = 6
Q_SCALE = HEAD_DIM ** -0.5 * math.log2(math.e)

LANES = 128
SUBLANES = 8
HALO = 2 * SUBLANES
VMEM_LIMIT = 56 * 1024 * 1024

PROJ_ROWS = 512
ATTN_BLOCK = 2048
ATTN_Q_SUB = 256
ATTN_K_SUB = 128
DECODE_PAGES = 16
DECODE_SLOTS = 3


def _lam_init(layer):
    return 0.8 - 0.6 * math.exp(-0.3 * layer)


def _lam(lq1_ref, lk1_ref, lq2_ref, lk2_ref, lam_init):
    a = jnp.sum(lq1_ref[...] * lk1_ref[...], axis=-1, keepdims=True)
    b = jnp.sum(lq2_ref[...] * lk2_ref[...], axis=-1, keepdims=True)
    return jnp.exp(a) - jnp.exp(b) + lam_init


def _silu(x):
    return x * jax.nn.sigmoid(x)


def _rms(x, g):
    return x * lax.rsqrt(jnp.mean(x * x, axis=-1, keepdims=True) + EPS) * g


def _rope_tables(pos):
    half = ROT_DIM // 2
    inv = ROPE_THETA ** (-(jnp.arange(half, dtype=F32) * 2.0) / ROT_DIM)
    ang = pos.astype(F32)[:, None] * inv[None, :]
    cos, sin = jnp.cos(ang), jnp.sin(ang)
    n = pos.shape[0]
    rest = HEAD_DIM - ROT_DIM
    zeros_h = jnp.zeros((n, half), F32)
    c = jnp.concatenate([cos, cos, jnp.ones((n, rest), F32)], axis=1)
    a = jnp.concatenate([-sin, zeros_h, jnp.zeros((n, rest), F32)], axis=1)
    b = jnp.concatenate([zeros_h, sin, jnp.zeros((n, rest), F32)], axis=1)
    rep = LANES // HEAD_DIM
    return tuple(jnp.tile(t, (1, rep)) for t in (c, a, b))


def _proj_kernel(x_ref, g_ref, w_ref, c_ref, a_ref, b_ref, *refs, for_prompt, n_prev):
    prev_k, prev_v = refs[:n_prev], refs[n_prev:2 * n_prev]
    u_ref, gp_ref, k_ref, v_ref, ga_ref, *q_refs = refs[2 * n_prev:]
    rows = x_ref.shape[0]
    xb = _rms(x_ref[...], g_ref[...]).astype(BF16)
    width = ATTN_WIDTH
    half = ROT_DIM // 2

    def mm(j):
        return jnp.dot(xb, w_ref[:, j * width:(j + 1) * width], preferred_element_type=F32)

    def rope(z, j):
        zc = z[:, j * LANES:(j + 1) * LANES]
        return (zc * c_ref[...] + pltpu.roll(zc, LANES - half, 1) * a_ref[...]
                + pltpu.roll(zc, half, 1) * b_ref[...])

    u_ref[...] = mm(0)
    gp_ref[...] = mm(1)
    q = mm(2)
    k = mm(3)
    v = mm(4)
    for p in range(n_prev):
        k_ref[p] = prev_k[p][...]
        v_ref[p] = prev_v[p][...]
    for j in range(N_HEADS):
        cols = slice(j * LANES, (j + 1) * LANES)
        qr = rope(q, j) * Q_SCALE
        kr = rope(k, j)
        if for_prompt:
            qt_ref, kb_ref, vt_ref = q_refs
            head_rows = pl.ds(j, rows, stride=N_HEADS)
            k_ref[n_prev, head_rows, :] = kr
            v_ref[n_prev, head_rows, :] = v[:, cols]
            qt_ref[cols, :] = qr.T.astype(BF16)
            kb_ref[:, cols] = kr.astype(BF16)
            vt_ref[cols, :] = v[:, cols].T.astype(BF16)
        else:
            k_ref[:, cols] = kr
            v_ref[:, cols] = v[:, cols]
            q_refs[0][:, cols] = qr
    ga_ref[...] = mm(5)


def _project(x, g_pre, w_in, tables, rows, for_prompt, prev_kv=()):
    t, d = x.shape
    n_blocks = t // rows
    n_tab = tables[0].shape[0] // rows
    n_prev = len(prev_kv) // 2
    width = ATTN_WIDTH
    row_spec = lambda w: pl.BlockSpec((rows, w), lambda i: (i, 0))
    tab_spec = pl.BlockSpec((rows, LANES), lambda i: (i % n_tab, 0))
    f32_out = jax.ShapeDtypeStruct((t, width), F32)
    if for_prompt:
        kv_spec = pl.BlockSpec((n_prev + 1, rows * N_HEADS, V_DIM), lambda i: (0, i, 0))
        kv_out = jax.ShapeDtypeStruct((n_prev + 1, t * N_HEADS, V_DIM), F32)
        t_spec = pl.BlockSpec((None, width, rows), lambda i: (i, 0, 0))
        t_out = jax.ShapeDtypeStruct((n_blocks, width, rows), BF16)
        out_specs = [row_spec(width)] * 2 + [kv_spec] * 2 + [row_spec(width), t_spec,
                                                              row_spec(width), t_spec]
        out_shape = [f32_out] * 2 + [kv_out] * 2 + [f32_out, t_out,
                                                    jax.ShapeDtypeStruct((t, width), BF16), t_out]
    else:
        out_specs = [row_spec(width)] * 6
        out_shape = [f32_out] * 6
    prev_spec = pl.BlockSpec((rows * N_HEADS, V_DIM), lambda i: (i, 0))
    return pl.pallas_call(
        functools.partial(_proj_kernel, for_prompt=for_prompt, n_prev=n_prev),
        grid=(n_blocks,),
        in_specs=[row_spec(d), pl.BlockSpec((1, d), lambda i: (0, 0)),
                  pl.BlockSpec((d, N_PROJ * width), lambda i: (0, 0),
                               pipeline_mode=pl.Buffered(1)),
                  tab_spec, tab_spec, tab_spec] + [prev_spec] * (2 * n_prev),
        out_specs=out_specs,
        out_shape=out_shape,
        compiler_params=pltpu.CompilerParams(dimension_semantics=("parallel",),
                                             vmem_limit_bytes=VMEM_LIMIT),
        name="proj",
    )(x, g_pre.reshape(1, d), w_in, *tables, *prev_kv)


def _attn_kernel(qt_ref, k_ref, vt_ref, ga_ref, sub_ref, lq1_ref, lk1_ref, lq2_ref, lk2_ref,
                 o_ref, m_ref, l_ref, acc_ref, qh_ref, *, blk, lam_init):
    tile = qt_ref.shape[2]
    qb = pl.program_id(2)
    n_q = blk // ATTN_Q_SUB
    n_k = blk // ATTN_K_SUB
    m_ref[...] = jnp.full(m_ref.shape, -jnp.inf, F32)
    l_ref[...] = jnp.zeros(l_ref.shape, F32)
    acc_ref[...] = jnp.zeros(acc_ref.shape, F32)
    feat = lax.broadcasted_iota(jnp.int32, (V_DIM, ATTN_Q_SUB), 0)
    for j in range(n_q):
        q0 = j * ATTN_Q_SUB
        qt = qt_ref[q0 // tile, :, q0 % tile:q0 % tile + ATTN_Q_SUB]
        qh_ref[j] = jnp.where(feat < HEAD_DIM, qt, jnp.zeros_like(qt))
        qh_ref[n_q + j] = jnp.where(feat >= HEAD_DIM, qt, jnp.zeros_like(qt))

    def unit(half, j, kb, kk, masked):
        idx = half * n_q + j
        k0 = kk * ATTN_K_SUB
        start = pl.multiple_of(kb * blk + k0, ATTN_K_SUB)
        st = jnp.dot(k_ref[pl.ds(start, ATTN_K_SUB), :], qh_ref[idx],
                     preferred_element_type=F32)
        if masked:
            key = k0 + lax.broadcasted_iota(jnp.int32, st.shape, 0)
            qi = j * ATTN_Q_SUB + lax.broadcasted_iota(jnp.int32, st.shape, 1)
            st = jnp.where(key <= qi, st, -jnp.inf)
        m_old = m_ref[idx]
        m_new = jnp.maximum(m_old, jnp.max(st, axis=0, keepdims=True))
        alpha = jnp.exp2(m_old - m_new)
        pt = jnp.exp2(st - m_new)
        l_ref[idx] = alpha * l_ref[idx] + jnp.sum(pt, axis=0, keepdims=True)
        vt = vt_ref[kb * (blk // tile) + k0 // tile, :, k0 % tile:k0 % tile + ATTN_K_SUB]
        acc_ref[idx] = alpha * acc_ref[idx] + jnp.dot(vt, pt.astype(BF16),
                                                      preferred_element_type=F32)
        m_ref[idx] = m_new

    def full_block(kb, carry):
        for kk in range(n_k):
            for half in range(2):
                for j in range(n_q):
                    unit(half, j, kb, kk, False)
        return carry

    lax.fori_loop(0, qb, full_block, 0)
    for kk in range(n_k):
        for half in range(2):
            for j in range(n_q):
                first_key, last_key = kk * ATTN_K_SUB, (kk + 1) * ATTN_K_SUB - 1
                first_q, last_q = j * ATTN_Q_SUB, (j + 1) * ATTN_Q_SUB - 1
                if first_key <= last_q:
                    unit(half, j, qb, kk, last_key > first_q)

    lam = _lam(lq1_ref, lk1_ref, lq2_ref, lk2_ref, lam_init)
    for j in range(n_q):
        o = (acc_ref[j] / l_ref[j] - lam * (acc_ref[n_q + j] / l_ref[n_q + j])).T
        rows = slice(j * ATTN_Q_SUB, (j + 1) * ATTN_Q_SUB)
        y = _rms(o, sub_ref[...]) * (1.0 - lam_init)
        o_ref[rows, :] = (y * _silu(ga_ref[rows, :])).astype(BF16)


def _prompt_attention(qt, kb, vt, ga, subln, lams, lam_init, batch, seq):
    t = kb.shape[0]
    tile = qt.shape[2]
    blk = ATTN_BLOCK
    nq = seq // blk
    n_tiles = 2 * (blk // ATTN_Q_SUB)
    row_spec = pl.BlockSpec((blk, V_DIM), lambda b, h, i: (b * nq + i, h))
    qt_spec = pl.BlockSpec((blk // tile, V_DIM, tile), lambda b, h, i: (b * nq + i, h, 0))
    k_spec = pl.BlockSpec((seq, V_DIM), lambda b, h, i: (b, h))
    vt_spec = pl.BlockSpec((seq // tile, V_DIM, tile), lambda b, h, i: (b, h, 0))
    vec = lambda n: pl.BlockSpec((1, n), lambda b, h, i: (0, 0))
    return pl.pallas_call(
        functools.partial(_attn_kernel, blk=blk, lam_init=lam_init),
        grid=(batch, N_HEADS, nq),
        in_specs=[qt_spec, k_spec, vt_spec, row_spec, vec(V_DIM)] + [vec(HEAD_DIM)] * 4,
        out_specs=row_spec,
        out_shape=jax.ShapeDtypeStruct((t, ATTN_WIDTH), BF16),
        scratch_shapes=[pltpu.VMEM((n_tiles, 1, ATTN_Q_SUB), F32),
                        pltpu.VMEM((n_tiles, 1, ATTN_Q_SUB), F32),
                        pltpu.VMEM((n_tiles, V_DIM, ATTN_Q_SUB), F32),
                        pltpu.VMEM((n_tiles, V_DIM, ATTN_Q_SUB), BF16)],
        compiler_params=pltpu.CompilerParams(
            dimension_semantics=("parallel", "parallel", "arbitrary"),
            vmem_limit_bytes=VMEM_LIMIT),
        name="prompt_attn",
    )(qt, kb, vt, ga, subln.reshape(1, V_DIM), *lams)


def _mix_out(x, pooled, gp_ref, attn_part, wmix_ref, ps_ref, wout_ref, gpost_ref):
    parts = []
    for g, pg in enumerate(pooled):
        cols = slice(g * POOL_GROUP, (g + 1) * POOL_GROUP)
        mixed = jnp.dot(pg.astype(BF16), wmix_ref[g], preferred_element_type=F32) * ps_ref[:, cols]
        parts.append((mixed * _silu(gp_ref[:, cols])).astype(BF16))
    mixed_all = jnp.concatenate(parts + [attn_part], axis=1)
    z = jnp.dot(mixed_all, wout_ref[...], preferred_element_type=F32)
    return x + _rms(z, gpost_ref[...])


def _finish_kernel(x_ref, u_ref, halo_ref, gp_ref, am_ref, wmix_ref, ps_ref, wout_ref, gpost_ref,
                   y_ref, buf_ref, *, rows, blocks_per_seq):
    blk_in_seq = pl.program_id(0) % blocks_per_seq
    buf_ref[0:HALO, :] = jnp.where(blk_in_seq == 0, 0.0, halo_ref[...])
    buf_ref[HALO:, :] = u_ref[...]
    pos = lax.broadcasted_iota(jnp.int32, (rows, 1), 0) + blk_in_seq * rows
    pooled = []
    for g, w in enumerate(POOL_WINDOWS):
        cols = slice(g * POOL_GROUP, (g + 1) * POOL_GROUP)
        tok = buf_ref[HALO:HALO + rows, cols]
        acc = tok
        for j in range(1, w):
            acc = acc + buf_ref[HALO - j:HALO - j + rows, cols]
        cnt = jnp.minimum(w, pos + 1).astype(F32)
        pooled.append(acc / cnt - tok)
    y_ref[...] = _mix_out(x_ref[...], pooled, gp_ref, am_ref[...], wmix_ref, ps_ref, wout_ref,
                          gpost_ref)


def _prompt_finish(x, u, gp, am, w_mix, pool_scale, w_out, g_post, seq, rows):
    t, d = x.shape
    blocks_per_seq = seq // rows
    halo_per_block = rows // HALO
    row_spec = lambda w: pl.BlockSpec((rows, w), lambda i: (i, 0))
    halo_spec = pl.BlockSpec((HALO, POOL_WIDTH),
                             lambda i: (jnp.maximum(i * halo_per_block - 1, 0), 0))
    const = lambda shape: pl.BlockSpec(shape, lambda i: (0,) * len(shape))
    return pl.pallas_call(
        functools.partial(_finish_kernel, rows=rows, blocks_per_seq=blocks_per_seq),
        grid=(t // rows,),
        in_specs=[row_spec(d), row_spec(POOL_WIDTH), halo_spec, row_spec(POOL_WIDTH),
                  row_spec(ATTN_WIDTH), const(w_mix.shape), const((1, POOL_WIDTH)),
                  const(w_out.shape), const((1, d))],
        out_specs=row_spec(d),
        out_shape=jax.ShapeDtypeStruct((t, d), F32),
        scratch_shapes=[pltpu.VMEM((HALO + rows, POOL_WIDTH), F32)],
        compiler_params=pltpu.CompilerParams(dimension_semantics=("parallel",),
                                             vmem_limit_bytes=VMEM_LIMIT),
        name="prompt_finish",
    )(x, u, u, gp, am, w_mix, pool_scale.reshape(1, POOL_WIDTH), w_out, g_post.reshape(1, d))


def _sample_finish_kernel(x_ref, r_ref, gp_ref, ao_ref, ga_ref, sub_ref, wmix_ref, ps_ref,
                          wout_ref, gpost_ref, y_ref, *, n_new, counts, lam_init):
    pooled = []
    for g, w in enumerate(POOL_WINDOWS):
        cols = slice(g * POOL_GROUP, (g + 1) * POOL_GROUP)
        per_step = []
        for i in range(n_new):
            tok = r_ref[POOL_BUF + i, :, cols]
            acc = tok
            for j in range(1, w):
                acc = acc + r_ref[POOL_BUF + i - j, :, cols]
            per_step.append(acc / counts[g][i] - tok)
        pooled.append(jnp.concatenate(per_step, axis=0))
    heads = []
    for h in range(N_HEADS):
        cols = slice(h * V_DIM, (h + 1) * V_DIM)
        y = _rms(ao_ref[:, cols], sub_ref[...]) * (1.0 - lam_init)
        heads.append((y * _silu(ga_ref[:, cols])).astype(BF16))
    attn_part = jnp.concatenate(heads, axis=1)
    y_ref[...] = _mix_out(x_ref[...], pooled, gp_ref, attn_part, wmix_ref, ps_ref, wout_ref,
                          gpost_ref)


def _sample_finish(x, rows_hist, gp, ao, ga, subln, w_mix, pool_scale, w_out, g_post,
                   n_new, past_len, lam_init):
    t, d = x.shape
    counts = tuple(tuple(float(min(w, past_len + i + 1)) for i in range(n_new))
                   for w in POOL_WINDOWS)
    return pl.pallas_call(
        functools.partial(_sample_finish_kernel, n_new=n_new, counts=counts, lam_init=lam_init),
        out_shape=jax.ShapeDtypeStruct((t, d), F32),
        compiler_params=pltpu.CompilerParams(vmem_limit_bytes=VMEM_LIMIT),
        name="sample_finish",
    )(x, rows_hist, gp, ao, ga, subln.reshape(1, V_DIM), w_mix,
      pool_scale.reshape(1, POOL_WIDTH), w_out, g_post.reshape(1, d))


def _decode_kernel(pt_ref, q_ref, kn_ref, vn_ref, ck_hbm, cv_hbm, lq1_ref, lk1_ref, lq2_ref, lk2_ref,
                   o_ref, kbuf, vbuf, sem, wq_ref, m_ref, l_ref, acc_ref, pad_ref,
                   *, n_new, pages, layer, lam_init):
    j = pl.program_id(1)
    n_steps = pl.num_programs(1)
    t = pl.program_id(0) * n_steps + j
    total = pl.num_programs(0) * n_steps

    def copies(step, slot):
        out = []
        for r in range(pages):
            page = pt_ref[step * pages + r]
            out.append(pltpu.make_async_copy(ck_hbm.at[layer, page], kbuf.at[slot, r], sem.at[0, slot]))
            out.append(pltpu.make_async_copy(cv_hbm.at[layer, page], vbuf.at[slot, r], sem.at[1, slot]))
        return out

    @pl.when(t == 0)
    def _():
        for step in range(DECODE_SLOTS - 1):
            for c in copies(step, step):
                c.start()

    ahead = t + (DECODE_SLOTS - 1)

    @pl.when(ahead < total)
    def _():
        for c in copies(ahead, ahead % DECODE_SLOTS):
            c.start()

    slot = t % DECODE_SLOTS
    for c in copies(t, slot):
        c.wait()
    k_refs = [kbuf.at[slot, r] for r in range(pages)]
    v_refs = [vbuf.at[slot, r] for r in range(pages)]
    qh = n_new * N_HEADS
    page_rows = PAGE_SIZE * N_HEADS
    nt = (((1,), (1,)), ((), ()))
    row = lax.broadcasted_iota(jnp.int32, (2 * qh, page_rows), 0)
    col = lax.broadcasted_iota(jnp.int32, (2 * qh, page_rows), 1)
    same_head = (row % N_HEADS) == (col % N_HEADS)

    def update(state, scores, vals):
        m, l, acc = state
        m_new = m
        for s in scores:
            m_new = jnp.maximum(m_new, jnp.max(s, axis=-1, keepdims=True))
        alpha = jnp.exp2(m - m_new)
        l = alpha * l
        acc = alpha * acc
        for s, v in zip(scores, vals):
            p = jnp.exp2(s - m_new)
            l = l + jnp.sum(p, axis=-1, keepdims=True)
            acc = acc + jnp.dot(p.astype(BF16), v, preferred_element_type=F32)
        return m_new, l, acc

    def load_state():
        return m_ref[...], l_ref[...], acc_ref[...]

    def store_state(state):
        m_ref[...], l_ref[...], acc_ref[...] = state

    @pl.when(j == 0)
    def _():
        q = q_ref[...]
        lane = lax.broadcasted_iota(jnp.int32, q.shape, 1)
        wq_ref[...] = jnp.concatenate([jnp.where(lane < HEAD_DIM, q, 0.0),
                                       jnp.where(lane >= HEAD_DIM, q, 0.0)], axis=0).astype(BF16)
        pad_ref[...] = jnp.zeros(pad_ref.shape, F32)
        pad_ref[0:qh, :] = kn_ref[...]
        s = lax.dot_general(wq_ref[...], pad_ref[...].astype(BF16), nt, preferred_element_type=F32)
        causal = (col // N_HEADS) <= (row % qh) // N_HEADS
        s = jnp.where(same_head, jnp.where(causal, s, -jnp.inf), -jnp.inf)
        pad_ref[0:qh, :] = vn_ref[...]
        empty = (jnp.full(m_ref.shape, -jnp.inf, F32), jnp.zeros(l_ref.shape, F32),
                 jnp.zeros(acc_ref.shape, F32))
        store_state(update(empty, [s], [pad_ref[...].astype(BF16)]))

    wq = wq_ref[...]
    scores = [jnp.where(same_head,
                        lax.dot_general(wq, kr[...].astype(BF16), nt, preferred_element_type=F32),
                        -jnp.inf) for kr in k_refs]
    store_state(update(load_state(), scores, [vr[...].astype(BF16) for vr in v_refs]))

    @pl.when(j == pl.num_programs(1) - 1)
    def _():
        lam = _lam(lq1_ref, lk1_ref, lq2_ref, lk2_ref, lam_init)
        n = acc_ref[...] / l_ref[...]
        o_ref[...] = n[:qh] - lam * n[qh:]


def _decode_attention(q, k_new, v_new, cache_k, cache_v, page_table, lams, layer, lam_init):
    db, qh, width = q.shape
    n_pages = page_table.shape[1]
    pages = DECODE_PAGES
    page_rows = cache_k.shape[2]
    n_steps = n_pages // pages
    assert db * n_steps >= DECODE_SLOTS - 1
    tok_spec = pl.BlockSpec((None, qh, width), lambda b, j, pt: (b, 0, 0))
    hbm_spec = pl.BlockSpec(memory_space=pl.ANY)
    vec = pl.BlockSpec((1, HEAD_DIM), lambda b, j, pt: (0, 0))
    ring = pltpu.VMEM((DECODE_SLOTS, pages, page_rows, width), F32)
    grid_spec = pltpu.PrefetchScalarGridSpec(
        num_scalar_prefetch=1,
        grid=(db, n_steps),
        in_specs=[tok_spec] * 3 + [hbm_spec] * 2 + [vec] * 4,
        out_specs=tok_spec,
        scratch_shapes=[ring, ring, pltpu.SemaphoreType.DMA((2, DECODE_SLOTS)),
                        pltpu.VMEM((2 * qh, width), BF16),
                        pltpu.VMEM((2 * qh, 1), F32),
                        pltpu.VMEM((2 * qh, 1), F32),
                        pltpu.VMEM((2 * qh, width), F32),
                        pltpu.VMEM((page_rows, width), F32)],
    )
    return pl.pallas_call(
        functools.partial(_decode_kernel, n_new=qh // N_HEADS, pages=pages, layer=layer,
                          lam_init=lam_init),
        grid_spec=grid_spec,
        out_shape=jax.ShapeDtypeStruct((db, qh, width), F32),
        compiler_params=pltpu.CompilerParams(dimension_semantics=("arbitrary", "arbitrary"),
                                             vmem_limit_bytes=VMEM_LIMIT),
        name="decode_attn",
    )(page_table.reshape(-1), q, k_new, v_new, cache_k, cache_v, *lams)


def kernel(x_prompt, x_sample, cache_k, cache_v, state_pool, page_table, norm_pre, norm_post,
           w_in, w_pool_mix, pool_scale, lambda_q1, lambda_k1, lambda_q2, lambda_k2, subln, w_out):
    batch, seq, d = x_prompt.shape
    db, n_new, _ = x_sample.shape
    depth = w_in.shape[0]
    n_phys = cache_k.shape[1]
    past_len = page_table.shape[1] * PAGE_SIZE
    assert seq % PROJ_ROWS == 0 and seq % ATTN_BLOCK == 0 and page_table.shape[1] % DECODE_PAGES == 0
    assert w_in.shape[2] == N_PROJ * ATTN_WIDTH and POOL_WIDTH == ATTN_WIDTH

    w_in_b = w_in.astype(BF16)
    w_out_b = w_out.astype(BF16)
    w_mix_b = w_pool_mix.astype(BF16)
    cache_k = cache_k.reshape(depth, n_phys, PAGE_SIZE * N_HEADS, V_DIM)
    cache_v = cache_v.reshape(depth, n_phys, PAGE_SIZE * N_HEADS, V_DIM)
    tables_p = _rope_tables(jnp.arange(seq))
    tables_s = _rope_tables(past_len + jnp.repeat(jnp.arange(n_new), db))
    xp = x_prompt.reshape(batch * seq, d)
    xs = x_sample.transpose(1, 0, 2).reshape(n_new * db, d)

    def to_batch_major(a):
        return (a.reshape(n_new, db, N_HEADS, V_DIM).transpose(1, 0, 2, 3)
                .reshape(db, n_new * N_HEADS, V_DIM))

    outs = [[] for _ in range(4)]
    prev_k, prev_v = [], []
    for layer in range(depth):
        lam_init = _lam_init(layer)
        lams = [v[layer].reshape(1, HEAD_DIM) for v in (lambda_q1, lambda_k1, lambda_q2, lambda_k2)]
        prev_kv = prev_k + prev_v if layer == depth - 1 else []
        u, gp, k, v, ga, qt, kb, vt = _project(xp, norm_pre[layer], w_in_b[layer], tables_p,
                                               PROJ_ROWS, True, prev_kv)
        am = _prompt_attention(qt, kb, vt, ga, subln[layer], lams, lam_init, batch, seq)
        xp = _prompt_finish(xp, u, gp, am, w_mix_b[layer], pool_scale[layer], w_out_b[layer],
                            norm_post[layer], seq, PROJ_ROWS)
        if layer == depth - 1:
            new_k_prompt = k.reshape(depth, batch, seq, N_HEADS, V_DIM)
            new_v_prompt = v.reshape(depth, batch, seq, N_HEADS, V_DIM)
        else:
            prev_k.append(k[0])
            prev_v.append(v[0])
        outs[0].append(u.reshape(batch, seq, POOL_WIDTH)[:, seq - POOL_BUF:])
        u, gp, k, v, ga, q = _project(xs, norm_pre[layer], w_in_b[layer], tables_s, n_new * db, False)
        ao = _decode_attention(to_batch_major(q), to_batch_major(k), to_batch_major(v),
                               cache_k, cache_v, page_table, lams, layer, lam_init)
        hist = jnp.concatenate([state_pool[layer].transpose(1, 0, 2),
                                u.reshape(n_new, db, POOL_WIDTH)], axis=0)
        ao = ao.reshape(db, n_new, ATTN_WIDTH).transpose(1, 0, 2).reshape(n_new * db, ATTN_WIDTH)
        xs = _sample_finish(xs, hist, gp, ao, ga,
                            subln[layer], w_mix_b[layer], pool_scale[layer], w_out_b[layer],
                            norm_post[layer], n_new, past_len, lam_init)
        outs[1].append(to_batch_major(k).reshape(db, n_new, N_HEADS, V_DIM))
        outs[2].append(to_batch_major(v).reshape(db, n_new, N_HEADS, V_DIM))
        outs[3].append(hist[n_new:].transpose(1, 0, 2))
    y_prompt = xp.reshape(batch, seq, d)
    y_sample = xs.reshape(n_new, db, d).transpose(1, 0, 2)
    new_pool_prompt, new_k_sample, new_v_sample, new_pool_sample = (jnp.stack(o) for o in outs)
    return (y_prompt, y_sample, new_k_prompt, new_v_prompt, new_pool_prompt,
            new_k_sample, new_v_sample, new_pool_sample)
```

```python
import functools
import math

import jax
import jax.numpy as jnp
from jax import lax
from jax.experimental import pallas as pl
from jax.experimental.pallas import tpu as pltpu

F32 = jnp.float32
BF16 = jnp.bfloat16

POOL_WINDOWS = (2, 4, 8, 16)
POOL_GROUP = 128
POOL_WIDTH = POOL_GROUP * len(POOL_WINDOWS)
POOL_BUF = max(POOL_WINDOWS) - 1
HEAD_DIM = 64
V_DIM = 2 * HEAD_DIM
N_HEADS = 4
ATTN_WIDTH = N_HEADS * V_DIM
ROT_DIM = HEAD_DIM // 4
ROPE_THETA = 500000.0
EPS = 1e-6
PAGE_SIZE = 128
N_PROJ = 6
Q_SCALE = HEAD_DIM ** -0.5 * math.log2(math.e)

LANES = 128
SUBLANES = 8
HALO = 2 * SUBLANES
VMEM_LIMIT = 56 * 1024 * 1024

PROJ_ROWS = 512
ATTN_Q_SUB = 256
ATTN_K_SUB = 128
ATTN_MASKED_TILE_COST = 1.2
ATTN_INIT_COST = 3.0
ATTN_FINISH_COST = 11.0
DECODE_PAGES = 16
DECODE_SLOTS = 3


def _lam_init(layer):
    return 0.8 - 0.6 * math.exp(-0.3 * layer)


def _lam(lq1_ref, lk1_ref, lq2_ref, lk2_ref, lam_init):
    a = jnp.sum(lq1_ref[...] * lk1_ref[...], axis=-1, keepdims=True)
    b = jnp.sum(lq2_ref[...] * lk2_ref[...], axis=-1, keepdims=True)
    return jnp.exp(a) - jnp.exp(b) + lam_init


def _silu(x):
    return x * jax.nn.sigmoid(x)


def _rms(x, g):
    return x * lax.rsqrt(jnp.mean(x * x, axis=-1, keepdims=True) + EPS) * g


def _rope_tables(pos):
    half = ROT_DIM // 2
    inv = ROPE_THETA ** (-(jnp.arange(half, dtype=F32) * 2.0) / ROT_DIM)
    ang = pos.astype(F32)[:, None] * inv[None, :]
    cos, sin = jnp.cos(ang), jnp.sin(ang)
    n = pos.shape[0]
    rest = HEAD_DIM - ROT_DIM
    zeros_h = jnp.zeros((n, half), F32)
    c = jnp.concatenate([cos, cos, jnp.ones((n, rest), F32)], axis=1)
    a = jnp.concatenate([-sin, zeros_h, jnp.zeros((n, rest), F32)], axis=1)
    b = jnp.concatenate([zeros_h, sin, jnp.zeros((n, rest), F32)], axis=1)
    rep = LANES // HEAD_DIM
    return tuple(jnp.tile(t, (1, rep)) for t in (c, a, b))


def _proj_kernel(x_ref, g_ref, w_ref, c_ref, a_ref, b_ref, *refs, for_prompt, n_prev):
    prev_k, prev_v = refs[:n_prev], refs[n_prev:2 * n_prev]
    u_ref, gp_ref, k_ref, v_ref, ga_ref, *q_refs = refs[2 * n_prev:]
    rows = x_ref.shape[0]
    xb = _rms(x_ref[...], g_ref[...]).astype(BF16)
    width = ATTN_WIDTH
    half = ROT_DIM // 2

    def mm(j):
        return jnp.dot(xb, w_ref[:, j * width:(j + 1) * width], preferred_element_type=F32)

    def rope(z, j):
        zc = z[:, j * LANES:(j + 1) * LANES]
        return (zc * c_ref[...] + pltpu.roll(zc, LANES - half, 1) * a_ref[...]
                + pltpu.roll(zc, half, 1) * b_ref[...])

    u_ref[...] = mm(0)
    gp_ref[...] = mm(1)
    q = mm(2)
    k = mm(3)
    v = mm(4)
    for p in range(n_prev):
        k_ref[p] = prev_k[p][...]
        v_ref[p] = prev_v[p][...]
    for j in range(N_HEADS):
        cols = slice(j * LANES, (j + 1) * LANES)
        qr = rope(q, j) * Q_SCALE
        kr = rope(k, j)
        if for_prompt:
            qt_ref, kb_ref, vt_ref = q_refs
            head_rows = pl.ds(j, rows, stride=N_HEADS)
            k_ref[n_prev, head_rows, :] = kr
            v_ref[n_prev, head_rows, :] = v[:, cols]
            qt_ref[cols, :] = qr.T.astype(BF16)
            kb_ref[:, cols] = kr.astype(BF16)
            vt_ref[cols, :] = v[:, cols].T.astype(BF16)
        else:
            k_ref[:, cols] = kr
            v_ref[:, cols] = v[:, cols]
            q_refs[0][:, cols] = qr
    ga_ref[...] = mm(5)


def _project(x, g_pre, w_in, tables, rows, for_prompt, prev_kv=()):
    t, d = x.shape
    n_blocks = t // rows
    n_tab = tables[0].shape[0] // rows
    n_prev = len(prev_kv) // 2
    width = ATTN_WIDTH
    row_spec = lambda w: pl.BlockSpec((rows, w), lambda i: (i, 0))
    tab_spec = pl.BlockSpec((rows, LANES), lambda i: (i % n_tab, 0))
    f32_out = jax.ShapeDtypeStruct((t, width), F32)
    if for_prompt:
        kv_spec = pl.BlockSpec((n_prev + 1, rows * N_HEADS, V_DIM), lambda i: (0, i, 0))
        kv_out = jax.ShapeDtypeStruct((n_prev + 1, t * N_HEADS, V_DIM), F32)
        t_spec = pl.BlockSpec((None, width, rows), lambda i: (i, 0, 0))
        t_out = jax.ShapeDtypeStruct((n_blocks, width, rows), BF16)
        out_specs = [row_spec(width)] * 2 + [kv_spec] * 2 + [row_spec(width), t_spec,
                                                              row_spec(width), t_spec]
        out_shape = [f32_out] * 2 + [kv_out] * 2 + [f32_out, t_out,
                                                    jax.ShapeDtypeStruct((t, width), BF16), t_out]
    else:
        out_specs = [row_spec(width)] * 6
        out_shape = [f32_out] * 6
    prev_spec = pl.BlockSpec((rows * N_HEADS, V_DIM), lambda i: (i, 0))
    return pl.pallas_call(
        functools.partial(_proj_kernel, for_prompt=for_prompt, n_prev=n_prev),
        grid=(n_blocks,),
        in_specs=[row_spec(d), pl.BlockSpec((1, d), lambda i: (0, 0)),
                  pl.BlockSpec((d, N_PROJ * width), lambda i: (0, 0),
                               pipeline_mode=pl.Buffered(1)),
                  tab_spec, tab_spec, tab_spec] + [prev_spec] * (2 * n_prev),
        out_specs=out_specs,
        out_shape=out_shape,
        compiler_params=pltpu.CompilerParams(dimension_semantics=("parallel",),
                                             vmem_limit_bytes=VMEM_LIMIT),
        name="proj",
    )(x, g_pre.reshape(1, d), w_in, *tables, *prev_kv)


def _attn_tiles(seq):
    tiles = []
    for kk in range(seq // ATTN_K_SUB):
        for half in range(2):
            for j in range(seq // ATTN_Q_SUB):
                first_key, last_key = kk * ATTN_K_SUB, (kk + 1) * ATTN_K_SUB - 1
                first_q, last_q = j * ATTN_Q_SUB, (j + 1) * ATTN_Q_SUB - 1
                if first_key <= last_q:
                    tiles.append((half, j, kk, last_key > first_q))
    return tiles


def _attn_phases(seq, n_phases):
    tiles = _attn_tiles(seq)
    cost = [ATTN_MASKED_TILE_COST if t[3] else 1.0 for t in tiles]
    budget = (ATTN_INIT_COST + sum(cost) + ATTN_FINISH_COST) / n_phases
    phases = [[] for _ in range(n_phases)]
    spent = ATTN_INIT_COST
    for t, c in zip(tiles, cost):
        phases[min(int((spent + c / 2) / budget), n_phases - 1)].append(t)
        spent += c
    return tuple(tuple(p) for p in phases)


def _attn_init(qt_ref, m_ref, l_ref, acc_ref, qh_ref):
    tile = qt_ref.shape[2]
    n_q = qh_ref.shape[0] // 2
    m_ref[...] = jnp.full(m_ref.shape, -jnp.inf, F32)
    l_ref[...] = jnp.zeros(l_ref.shape, F32)
    acc_ref[...] = jnp.zeros(acc_ref.shape, F32)
    feat = lax.broadcasted_iota(jnp.int32, (V_DIM, ATTN_Q_SUB), 0)
    for j in range(n_q):
        q0 = j * ATTN_Q_SUB
        qt = qt_ref[q0 // tile, :, q0 % tile:q0 % tile + ATTN_Q_SUB]
        qh_ref[j] = jnp.where(feat < HEAD_DIM, qt, jnp.zeros_like(qt))
        qh_ref[n_q + j] = jnp.where(feat >= HEAD_DIM, qt, jnp.zeros_like(qt))


def _attn_tile(half, j, kk, masked, k_ref, vt_ref, m_ref, l_ref, acc_ref, qh_ref):
    tile = vt_ref.shape[2]
    idx = half * (qh_ref.shape[0] // 2) + j
    k0 = kk * ATTN_K_SUB
    st = jnp.dot(k_ref[k0:k0 + ATTN_K_SUB, :], qh_ref[idx], preferred_element_type=F32)
    if masked:
        key = k0 + lax.broadcasted_iota(jnp.int32, st.shape, 0)
        qi = j * ATTN_Q_SUB + lax.broadcasted_iota(jnp.int32, st.shape, 1)
        st = jnp.where(key <= qi, st, -jnp.inf)
    m_old = m_ref[idx]
    m_new = jnp.maximum(m_old, jnp.max(st, axis=0, keepdims=True))
    alpha = jnp.exp2(m_old - m_new)
    pt = jnp.exp2(st - m_new)
    l_ref[idx] = alpha * l_ref[idx] + jnp.sum(pt, axis=0, keepdims=True)
    vt = vt_ref[k0 // tile, :, k0 % tile:k0 % tile + ATTN_K_SUB]
    acc_ref[idx] = alpha * acc_ref[idx] + jnp.dot(vt, pt.astype(BF16), preferred_element_type=F32)
    m_ref[idx] = m_new


def _attn_finish(lam, ga_ref, sub_ref, o_ref, l_ref, acc_ref, lam_init):
    n_q = acc_ref.shape[0] // 2
    for j in range(n_q):
        o = (acc_ref[j] / l_ref[j] - lam * (acc_ref[n_q + j] / l_ref[n_q + j])).T
        rows = slice(j * ATTN_Q_SUB, (j + 1) * ATTN_Q_SUB)
        y = _rms(o, sub_ref[...]) * (1.0 - lam_init)
        o_ref[rows, :] = (y * _silu(ga_ref[rows, :])).astype(BF16)


def _mix_out(x, pooled, gp_ref, attn_part, wmix_ref, ps_ref, wout_ref, gpost_ref):
    parts = []
    for g, pg in enumerate(pooled):
        cols = slice(g * POOL_GROUP, (g + 1) * POOL_GROUP)
        mixed = jnp.dot(pg.astype(BF16), wmix_ref[g], preferred_element_type=F32) * ps_ref[:, cols]
        parts.append((mixed * _silu(gp_ref[:, cols])).astype(BF16))
    mixed_all = jnp.concatenate(parts + [attn_part], axis=1)
    z = jnp.dot(mixed_all, wout_ref[...], preferred_element_type=F32)
    return x + _rms(z, gpost_ref[...])


def _finish_kernel(x_ref, u_ref, halo_ref, gp_ref, am_ref, wmix_ref, ps_ref, wout_ref, gpost_ref,
                   y_ref, buf_ref, *, rows, blocks_per_seq):
    blk_in_seq = pl.program_id(0) % blocks_per_seq
    buf_ref[0:HALO, :] = jnp.where(blk_in_seq == 0, 0.0, halo_ref[...])
    buf_ref[HALO:, :] = u_ref[...]
    pos = lax.broadcasted_iota(jnp.int32, (rows, 1), 0) + blk_in_seq * rows
    pooled = []
    for g, w in enumerate(POOL_WINDOWS):
        cols = slice(g * POOL_GROUP, (g + 1) * POOL_GROUP)
        tok = buf_ref[HALO:HALO + rows, cols]
        acc = tok
        for j in range(1, w):
            acc = acc + buf_ref[HALO - j:HALO - j + rows, cols]
        cnt = jnp.minimum(w, pos + 1).astype(F32)
        pooled.append(acc / cnt - tok)
    y_ref[...] = _mix_out(x_ref[...], pooled, gp_ref, am_ref[...], wmix_ref, ps_ref, wout_ref,
                          gpost_ref)


def _prompt_finish(x, u, gp, am, w_mix, pool_scale, w_out, g_post, seq, rows):
    t, d = x.shape
    blocks_per_seq = seq // rows
    halo_per_block = rows // HALO
    row_spec = lambda w: pl.BlockSpec((rows, w), lambda i: (i, 0))
    halo_spec = pl.BlockSpec((HALO, POOL_WIDTH),
                             lambda i: (jnp.maximum(i * halo_per_block - 1, 0), 0))
    const = lambda shape: pl.BlockSpec(shape, lambda i: (0,) * len(shape))
    return pl.pallas_call(
        functools.partial(_finish_kernel, rows=rows, blocks_per_seq=blocks_per_seq),
        grid=(t // rows,),
        in_specs=[row_spec(d), row_spec(POOL_WIDTH), halo_spec, row_spec(POOL_WIDTH),
                  row_spec(ATTN_WIDTH), const(w_mix.shape), const((1, POOL_WIDTH)),
                  const(w_out.shape), const((1, d))],
        out_specs=row_spec(d),
        out_shape=jax.ShapeDtypeStruct((t, d), F32),
        scratch_shapes=[pltpu.VMEM((HALO + rows, POOL_WIDTH), F32)],
        compiler_params=pltpu.CompilerParams(dimension_semantics=("parallel",),
                                             vmem_limit_bytes=VMEM_LIMIT),
        name="prompt_finish",
    )(x, u, u, gp, am, w_mix, pool_scale.reshape(1, POOL_WIDTH), w_out, g_post.reshape(1, d))


def _sample_finish_kernel(x_ref, r_ref, gp_ref, ao_ref, ga_ref, sub_ref, wmix_ref, ps_ref,
                          wout_ref, gpost_ref, y_ref, *, n_new, counts, lam_init):
    pooled = []
    for g, w in enumerate(POOL_WINDOWS):
        cols = slice(g * POOL_GROUP, (g + 1) * POOL_GROUP)
        per_step = []
        for i in range(n_new):
            tok = r_ref[POOL_BUF + i, :, cols]
            acc = tok
            for j in range(1, w):
                acc = acc + r_ref[POOL_BUF + i - j, :, cols]
            per_step.append(acc / counts[g][i] - tok)
        pooled.append(jnp.concatenate(per_step, axis=0))
    heads = []
    for h in range(N_HEADS):
        cols = slice(h * V_DIM, (h + 1) * V_DIM)
        y = _rms(ao_ref[:, cols], sub_ref[...]) * (1.0 - lam_init)
        heads.append((y * _silu(ga_ref[:, cols])).astype(BF16))
    attn_part = jnp.concatenate(heads, axis=1)
    y_ref[...] = _mix_out(x_ref[...], pooled, gp_ref, attn_part, wmix_ref, ps_ref, wout_ref,
                          gpost_ref)


def _sample_finish(x, rows_hist, gp, ao, ga, subln, w_mix, pool_scale, w_out, g_post,
                   n_new, past_len, lam_init):
    t, d = x.shape
    counts = tuple(tuple(float(min(w, past_len + i + 1)) for i in range(n_new))
                   for w in POOL_WINDOWS)
    return pl.pallas_call(
        functools.partial(_sample_finish_kernel, n_new=n_new, counts=counts, lam_init=lam_init),
        out_shape=jax.ShapeDtypeStruct((t, d), F32),
        compiler_params=pltpu.CompilerParams(vmem_limit_bytes=VMEM_LIMIT),
        name="sample_finish",
    )(x, rows_hist, gp, ao, ga, subln.reshape(1, V_DIM), w_mix,
      pool_scale.reshape(1, POOL_WIDTH), w_out, g_post.reshape(1, d))


def _decode_step(t, total, steps_per_batch, lam, pt_ref, q_ref, kn_ref, vn_ref, ck_hbm, cv_hbm,
                 o_ref, kbuf, vbuf, sem, wq_ref, m_ref, l_ref, acc_ref, pad_ref,
                 *, n_new, pages, layer):
    j = t % steps_per_batch

    def copies(step, slot):
        out = []
        for r in range(pages):
            page = pt_ref[step * pages + r]
            out.append(pltpu.make_async_copy(ck_hbm.at[layer, page], kbuf.at[slot, r], sem.at[0, slot]))
            out.append(pltpu.make_async_copy(cv_hbm.at[layer, page], vbuf.at[slot, r], sem.at[1, slot]))
        return out

    @pl.when(t == 0)
    def _():
        for step in range(DECODE_SLOTS - 1):
            for c in copies(step, step):
                c.start()

    ahead = t + (DECODE_SLOTS - 1)

    @pl.when(ahead < total)
    def _():
        for c in copies(ahead, ahead % DECODE_SLOTS):
            c.start()

    slot = t % DECODE_SLOTS
    for c in copies(t, slot):
        c.wait()
    k_refs = [kbuf.at[slot, r] for r in range(pages)]
    v_refs = [vbuf.at[slot, r] for r in range(pages)]
    qh = n_new * N_HEADS
    page_rows = PAGE_SIZE * N_HEADS
    nt = (((1,), (1,)), ((), ()))
    row = lax.broadcasted_iota(jnp.int32, (2 * qh, page_rows), 0)
    col = lax.broadcasted_iota(jnp.int32, (2 * qh, page_rows), 1)
    same_head = (row % N_HEADS) == (col % N_HEADS)

    def update(state, scores, vals):
        m, l, acc = state
        m_new = m
        for s in scores:
            m_new = jnp.maximum(m_new, jnp.max(s, axis=-1, keepdims=True))
        alpha = jnp.exp2(m - m_new)
        l = alpha * l
        acc = alpha * acc
        for s, v in zip(scores, vals):
            p = jnp.exp2(s - m_new)
            l = l + jnp.sum(p, axis=-1, keepdims=True)
            acc = acc + jnp.dot(p.astype(BF16), v, preferred_element_type=F32)
        return m_new, l, acc

    def load_state():
        return m_ref[...], l_ref[...], acc_ref[...]

    def store_state(state):
        m_ref[...], l_ref[...], acc_ref[...] = state

    @pl.when(j == 0)
    def _():
        q = q_ref[...]
        lane = lax.broadcasted_iota(jnp.int32, q.shape, 1)
        wq_ref[...] = jnp.concatenate([jnp.where(lane < HEAD_DIM, q, 0.0),
                                       jnp.where(lane >= HEAD_DIM, q, 0.0)], axis=0).astype(BF16)
        pad_ref[...] = jnp.zeros(pad_ref.shape, F32)
        pad_ref[0:qh, :] = kn_ref[...]
        s = lax.dot_general(wq_ref[...], pad_ref[...].astype(BF16), nt, preferred_element_type=F32)
        causal = (col // N_HEADS) <= (row % qh) // N_HEADS
        s = jnp.where(same_head, jnp.where(causal, s, -jnp.inf), -jnp.inf)
        pad_ref[0:qh, :] = vn_ref[...]
        empty = (jnp.full(m_ref.shape, -jnp.inf, F32), jnp.zeros(l_ref.shape, F32),
                 jnp.zeros(acc_ref.shape, F32))
        store_state(update(empty, [s], [pad_ref[...].astype(BF16)]))

    wq = wq_ref[...]
    scores = [jnp.where(same_head,
                        lax.dot_general(wq, kr[...].astype(BF16), nt, preferred_element_type=F32),
                        -jnp.inf) for kr in k_refs]
    store_state(update(load_state(), scores, [vr[...].astype(BF16) for vr in v_refs]))

    @pl.when(j == steps_per_batch - 1)
    def _():
        n = acc_ref[...] / l_ref[...]
        o_ref[...] = n[:qh] - lam * n[qh:]


def _attention_kernel(pt_ref, q_ref, kn_ref, vn_ref, ck_hbm, cv_hbm, qt_ref, k_ref, vt_ref, ga_ref,
                      sub_ref, lq1_ref, lk1_ref, lq2_ref, lk2_ref, dec_ref, att_ref,
                      kbuf, vbuf, sem, wq_ref, dm_ref, dl_ref, dacc_ref, pad_ref,
                      am_ref, al_ref, aacc_ref, qh_ref,
                      *, n_new, pages, layer, lam_init, steps_per_batch, phases):
    phase = pl.program_id(1)
    n_phases = pl.num_programs(1)
    t = pl.program_id(0) * n_phases + phase
    lam = _lam(lq1_ref, lk1_ref, lq2_ref, lk2_ref, lam_init)
    _decode_step(t, pl.num_programs(0) * n_phases, steps_per_batch, lam, pt_ref, q_ref, kn_ref,
                 vn_ref, ck_hbm, cv_hbm, dec_ref, kbuf, vbuf, sem, wq_ref, dm_ref, dl_ref,
                 dacc_ref, pad_ref, n_new=n_new, pages=pages, layer=layer)

    @pl.when(phase == 0)
    def _():
        _attn_init(qt_ref, am_ref, al_ref, aacc_ref, qh_ref)

    for p, tiles in enumerate(phases):
        @pl.when(phase == p)
        def _(tiles=tiles):
            for tile in tiles:
                _attn_tile(*tile, k_ref, vt_ref, am_ref, al_ref, aacc_ref, qh_ref)

    @pl.when(phase == n_phases - 1)
    def _():
        _attn_finish(lam, ga_ref, sub_ref, att_ref, al_ref, aacc_ref, lam_init)


def _attention(q, k_new, v_new, cache_k, cache_v, page_table, qt, kb, vt, ga, subln, lams,
               layer, lam_init, batch, seq):
    db, qh, width = q.shape
    t = kb.shape[0]
    tile = qt.shape[2]
    pages = DECODE_PAGES
    page_rows = cache_k.shape[2]
    steps_per_batch = page_table.shape[1] // pages
    n_seqs = batch * N_HEADS
    n_phases = db * steps_per_batch // n_seqs
    assert n_phases * n_seqs == db * steps_per_batch and n_phases * n_seqs >= DECODE_SLOTS - 1
    phases = _attn_phases(seq, n_phases)
    n_state = 2 * (seq // ATTN_Q_SUB)

    tok_spec = pl.BlockSpec((None, qh, width),
                            lambda g, p, pt: ((g * n_phases + p) // steps_per_batch, 0, 0))
    hbm_spec = pl.BlockSpec(memory_space=pl.ANY)
    seq_spec = pl.BlockSpec((seq, V_DIM), lambda g, p, pt: (g // N_HEADS, g % N_HEADS))
    t_spec = pl.BlockSpec((seq // tile, V_DIM, tile),
                          lambda g, p, pt: (g // N_HEADS, g % N_HEADS, 0))
    vec = lambda n: pl.BlockSpec((1, n), lambda g, p, pt: (0, 0))
    ring = pltpu.VMEM((DECODE_SLOTS, pages, page_rows, width), F32)
    grid_spec = pltpu.PrefetchScalarGridSpec(
        num_scalar_prefetch=1,
        grid=(n_seqs, n_phases),
        in_specs=([tok_spec] * 3 + [hbm_spec] * 2 + [t_spec, seq_spec, t_spec, seq_spec,
                                                     vec(V_DIM)] + [vec(HEAD_DIM)] * 4),
        out_specs=[tok_spec, seq_spec],
        scratch_shapes=[ring, ring, pltpu.SemaphoreType.DMA((2, DECODE_SLOTS)),
                        pltpu.VMEM((2 * qh, width), BF16),
                        pltpu.VMEM((2 * qh, 1), F32),
                        pltpu.VMEM((2 * qh, 1), F32),
                        pltpu.VMEM((2 * qh, width), F32),
                        pltpu.VMEM((page_rows, width), F32),
                        pltpu.VMEM((n_state, 1, ATTN_Q_SUB), F32),
                        pltpu.VMEM((n_state, 1, ATTN_Q_SUB), F32),
                        pltpu.VMEM((n_state, V_DIM, ATTN_Q_SUB), F32),
                        pltpu.VMEM((n_state, V_DIM, ATTN_Q_SUB), BF16)],
    )
    return pl.pallas_call(
        functools.partial(_attention_kernel, n_new=qh // N_HEADS, pages=pages, layer=layer,
                          lam_init=lam_init, steps_per_batch=steps_per_batch, phases=phases),
        grid_spec=grid_spec,
        out_shape=[jax.ShapeDtypeStruct((db, qh, width), F32),
                   jax.ShapeDtypeStruct((t, ATTN_WIDTH), BF16)],
        compiler_params=pltpu.CompilerParams(dimension_semantics=("arbitrary", "arbitrary"),
                                             vmem_limit_bytes=VMEM_LIMIT),
        name="attention",
    )(page_table.reshape(-1), q, k_new, v_new, cache_k, cache_v, qt, kb, vt, ga,
      subln.reshape(1, V_DIM), *lams)


def kernel(x_prompt, x_sample, cache_k, cache_v, state_pool, page_table, norm_pre, norm_post,
           w_in, w_pool_mix, pool_scale, lambda_q1, lambda_k1, lambda_q2, lambda_k2, subln, w_out):
    batch, seq, d = x_prompt.shape
    db, n_new, _ = x_sample.shape
    depth = w_in.shape[0]
    n_phys = cache_k.shape[1]
    past_len = page_table.shape[1] * PAGE_SIZE
    assert seq % PROJ_ROWS == 0 and seq % ATTN_Q_SUB == 0 and page_table.shape[1] % DECODE_PAGES == 0
    assert w_in.shape[2] == N_PROJ * ATTN_WIDTH and POOL_WIDTH == ATTN_WIDTH

    w_in_b = w_in.astype(BF16)
    w_out_b = w_out.astype(BF16)
    w_mix_b = w_pool_mix.astype(BF16)
    cache_k = cache_k.reshape(depth, n_phys, PAGE_SIZE * N_HEADS, V_DIM)
    cache_v = cache_v.reshape(depth, n_phys, PAGE_SIZE * N_HEADS, V_DIM)
    tables_p = _rope_tables(jnp.arange(seq))
    tables_s = _rope_tables(past_len + jnp.repeat(jnp.arange(n_new), db))
    xp = x_prompt.reshape(batch * seq, d)
    xs = x_sample.transpose(1, 0, 2).reshape(n_new * db, d)

    def to_batch_major(a):
        return (a.reshape(n_new, db, N_HEADS, V_DIM).transpose(1, 0, 2, 3)
                .reshape(db, n_new * N_HEADS, V_DIM))

    outs = [[] for _ in range(4)]
    prev_k, prev_v = [], []
    for layer in range(depth):
        lam_init = _lam_init(layer)
        lams = [v[layer].reshape(1, HEAD_DIM) for v in (lambda_q1, lambda_k1, lambda_q2, lambda_k2)]
        prev_kv = prev_k + prev_v if layer == depth - 1 else []
        up, gpp, kp, vp, gap, qt, kb, vt = _project(xp, norm_pre[layer], w_in_b[layer], tables_p,
                                                    PROJ_ROWS, True, prev_kv)
        u, gp, k, v, ga, q = _project(xs, norm_pre[layer], w_in_b[layer], tables_s, n_new * db, False)
        ao, am = _attention(to_batch_major(q), to_batch_major(k), to_batch_major(v), cache_k,
                            cache_v, page_table, qt, kb, vt, gap, subln[layer], lams, layer,
                            lam_init, batch, seq)
        xp = _prompt_finish(xp, up, gpp, am, w_mix_b[layer], pool_scale[layer], w_out_b[layer],
                            norm_post[layer], seq, PROJ_ROWS)
        if layer == depth - 1:
            new_k_prompt = kp.reshape(depth, batch, seq, N_HEADS, V_DIM)
            new_v_prompt = vp.reshape(depth, batch, seq, N_HEADS, V_DIM)
        else:
            prev_k.append(kp[0])
            prev_v.append(vp[0])
        outs[0].append(up.reshape(batch, seq, POOL_WIDTH)[:, seq - POOL_BUF:])
        hist = jnp.concatenate([state_pool[layer].transpose(1, 0, 2),
                                u.reshape(n_new, db, POOL_WIDTH)], axis=0)
        ao = ao.reshape(db, n_new, ATTN_WIDTH).transpose(1, 0, 2).reshape(n_new * db, ATTN_WIDTH)
        xs = _sample_finish(xs, hist, gp, ao, ga,
                            subln[layer], w_mix_b[layer], pool_scale[layer], w_out_b[layer],
                            norm_post[layer], n_new, past_len, lam_init)
        outs[1].append(to_batch_major(k).reshape(db, n_new, N_HEADS, V_DIM))
        outs[2].append(to_batch_major(v).reshape(db, n_new, N_HEADS, V_DIM))
        outs[3].append(hist[n_new:].transpose(1, 0, 2))
    y_prompt = xp.reshape(batch, seq, d)
    y_sample = xs.reshape(n_new, db, d).transpose(1, 0, 2)
    new_pool_prompt, new_k_sample, new_v_sample, new_pool_sample = (jnp.stack(o) for o in outs)
    return (y_prompt, y_sample, new_k_prompt, new_v_prompt, new_pool_prompt,
            new_k_sample, new_v_sample, new_pool_sample)
```

```python
import functools
import math

import jax
import jax.numpy as jnp
from jax import lax
from jax.experimental import pallas as pl
from jax.experimental.pallas import tpu as pltpu

F32 = jnp.float32
BF16 = jnp.bfloat16

POOL_WINDOWS = (2, 4, 8, 16)
POOL_GROUP = 128
POOL_WIDTH = POOL_GROUP * len(POOL_WINDOWS)
POOL_BUF = max(POOL_WINDOWS) - 1
HEAD_DIM = 64
V_DIM = 2 * HEAD_DIM
N_HEADS = 4
ATTN_WIDTH = N_HEADS * V_DIM
ROT_DIM = HEAD_DIM // 4
ROPE_THETA = 500000.0
EPS = 1e-6
PAGE_SIZE = 128
N_PROJ = 6
Q_SCALE = HEAD_DIM ** -0.5 * math.log2(math.e)

LANES = 128
SUBLANES = 8
HALO = 2 * SUBLANES
VMEM_LIMIT = 56 * 1024 * 1024

PROJ_ROWS = 512
ATTN_Q_SUB = 256
ATTN_K_SUB = 128
ATTN_MASKED_TILE_COST = 1.2
ATTN_INIT_COST = 3.0
ATTN_FINISH_COST = 11.0
DECODE_PAGES = 16
DECODE_SLOTS = 4
ATTN_RUNS = 2


def _lam_init(layer):
    return 0.8 - 0.6 * math.exp(-0.3 * layer)


def _lam(lq1_ref, lk1_ref, lq2_ref, lk2_ref, lam_init):
    a = jnp.sum(lq1_ref[...] * lk1_ref[...], axis=-1, keepdims=True)
    b = jnp.sum(lq2_ref[...] * lk2_ref[...], axis=-1, keepdims=True)
    return jnp.exp(a) - jnp.exp(b) + lam_init


def _silu(x):
    return x * jax.nn.sigmoid(x)


def _rms(x, g):
    return x * lax.rsqrt(jnp.mean(x * x, axis=-1, keepdims=True) + EPS) * g


def _rope_tables(pos):
    half = ROT_DIM // 2
    inv = ROPE_THETA ** (-(jnp.arange(half, dtype=F32) * 2.0) / ROT_DIM)
    ang = pos.astype(F32)[:, None] * inv[None, :]
    cos, sin = jnp.cos(ang), jnp.sin(ang)
    n = pos.shape[0]
    rest = HEAD_DIM - ROT_DIM
    zeros_h = jnp.zeros((n, half), F32)
    c = jnp.concatenate([cos, cos, jnp.ones((n, rest), F32)], axis=1)
    a = jnp.concatenate([-sin, zeros_h, jnp.zeros((n, rest), F32)], axis=1)
    b = jnp.concatenate([zeros_h, sin, jnp.zeros((n, rest), F32)], axis=1)
    rep = LANES // HEAD_DIM
    return tuple(jnp.tile(t, (1, rep)) for t in (c, a, b))


def _proj_kernel(x_ref, g_ref, w_ref, c_ref, a_ref, b_ref, *refs, for_prompt, n_prev):
    prev_k, prev_v = refs[:n_prev], refs[n_prev:2 * n_prev]
    u_ref, gp_ref, k_ref, v_ref, ga_ref, *q_refs = refs[2 * n_prev:]
    rows = x_ref.shape[0]
    xb = _rms(x_ref[...], g_ref[...]).astype(BF16)
    width = ATTN_WIDTH
    half = ROT_DIM // 2

    def mm(j):
        return jnp.dot(xb, w_ref[:, j * width:(j + 1) * width], preferred_element_type=F32)

    def rope(z, j):
        zc = z[:, j * LANES:(j + 1) * LANES]
        return (zc * c_ref[...] + pltpu.roll(zc, LANES - half, 1) * a_ref[...]
                + pltpu.roll(zc, half, 1) * b_ref[...])

    u_ref[...] = mm(0)
    gp_ref[...] = mm(1)
    q = mm(2)
    k = mm(3)
    v = mm(4)
    for p in range(n_prev):
        k_ref[p] = prev_k[p][...]
        v_ref[p] = prev_v[p][...]
    for j in range(N_HEADS):
        cols = slice(j * LANES, (j + 1) * LANES)
        qr = rope(q, j) * Q_SCALE
        kr = rope(k, j)
        if for_prompt:
            qt_ref, kb_ref, vt_ref = q_refs
            head_rows = pl.ds(j, rows, stride=N_HEADS)
            k_ref[n_prev, head_rows, :] = kr
            v_ref[n_prev, head_rows, :] = v[:, cols]
            qt_ref[cols, :] = qr.T.astype(BF16)
            kb_ref[:, cols] = kr.astype(BF16)
            vt_ref[cols, :] = v[:, cols].T.astype(BF16)
        else:
            k_ref[:, cols] = kr
            v_ref[:, cols] = v[:, cols]
            q_refs[0][:, cols] = qr
    ga_ref[...] = mm(5)


def _project(x, g_pre, w_in, tables, rows, for_prompt, prev_kv=()):
    t, d = x.shape
    n_blocks = t // rows
    n_tab = tables[0].shape[0] // rows
    n_prev = len(prev_kv) // 2
    width = ATTN_WIDTH
    row_spec = lambda w: pl.BlockSpec((rows, w), lambda i: (i, 0))
    tab_spec = pl.BlockSpec((rows, LANES), lambda i: (i % n_tab, 0))
    f32_out = jax.ShapeDtypeStruct((t, width), F32)
    if for_prompt:
        kv_spec = pl.BlockSpec((n_prev + 1, rows * N_HEADS, V_DIM), lambda i: (0, i, 0))
        kv_out = jax.ShapeDtypeStruct((n_prev + 1, t * N_HEADS, V_DIM), F32)
        t_spec = pl.BlockSpec((None, width, rows), lambda i: (i, 0, 0))
        t_out = jax.ShapeDtypeStruct((n_blocks, width, rows), BF16)
        out_specs = [row_spec(width)] * 2 + [kv_spec] * 2 + [row_spec(width), t_spec,
                                                              row_spec(width), t_spec]
        out_shape = [f32_out] * 2 + [kv_out] * 2 + [f32_out, t_out,
                                                    jax.ShapeDtypeStruct((t, width), BF16), t_out]
    else:
        out_specs = [row_spec(width)] * 6
        out_shape = [f32_out] * 6
    prev_spec = pl.BlockSpec((rows * N_HEADS, V_DIM), lambda i: (i, 0))
    return pl.pallas_call(
        functools.partial(_proj_kernel, for_prompt=for_prompt, n_prev=n_prev),
        grid=(n_blocks,),
        in_specs=[row_spec(d), pl.BlockSpec((1, d), lambda i: (0, 0)),
                  pl.BlockSpec((d, N_PROJ * width), lambda i: (0, 0),
                               pipeline_mode=pl.Buffered(1)),
                  tab_spec, tab_spec, tab_spec] + [prev_spec] * (2 * n_prev),
        out_specs=out_specs,
        out_shape=out_shape,
        compiler_params=pltpu.CompilerParams(dimension_semantics=("parallel",),
                                             vmem_limit_bytes=VMEM_LIMIT),
        name="proj",
    )(x, g_pre.reshape(1, d), w_in, *tables, *prev_kv)


def _attn_tiles(seq):
    tiles = []
    for kk in range(seq // ATTN_K_SUB):
        for half in range(2):
            for j in range(seq // ATTN_Q_SUB):
                first_key, last_key = kk * ATTN_K_SUB, (kk + 1) * ATTN_K_SUB - 1
                first_q, last_q = j * ATTN_Q_SUB, (j + 1) * ATTN_Q_SUB - 1
                if first_key <= last_q:
                    tiles.append((half, j, kk, last_key > first_q))
    return tiles


def _attn_phases(seq, n_phases):
    tiles = _attn_tiles(seq)
    cost = [ATTN_MASKED_TILE_COST if t[3] else 1.0 for t in tiles]
    budget = (ATTN_INIT_COST + sum(cost) + ATTN_FINISH_COST) / n_phases
    phases = [[] for _ in range(n_phases)]
    spent = ATTN_INIT_COST
    for t, c in zip(tiles, cost):
        phases[min(int((spent + c / 2) / budget), n_phases - 1)].append(t)
        spent += c
    return tuple(tuple(p) for p in phases)


def _attn_init(qt_ref, m_ref, l_ref, acc_ref, qh_ref):
    tile = qt_ref.shape[2]
    n_q = qh_ref.shape[0] // 2
    m_ref[...] = jnp.full(m_ref.shape, -jnp.inf, F32)
    l_ref[...] = jnp.zeros(l_ref.shape, F32)
    acc_ref[...] = jnp.zeros(acc_ref.shape, F32)
    feat = lax.broadcasted_iota(jnp.int32, (V_DIM, ATTN_Q_SUB), 0)
    for j in range(n_q):
        q0 = j * ATTN_Q_SUB
        qt = qt_ref[q0 // tile, :, q0 % tile:q0 % tile + ATTN_Q_SUB]
        qh_ref[j] = jnp.where(feat < HEAD_DIM, qt, jnp.zeros_like(qt))
        qh_ref[n_q + j] = jnp.where(feat >= HEAD_DIM, qt, jnp.zeros_like(qt))


def _attn_tile(half, j, kk, masked, k_ref, vt_ref, m_ref, l_ref, acc_ref, qh_ref):
    tile = vt_ref.shape[2]
    idx = half * (qh_ref.shape[0] // 2) + j
    k0 = kk * ATTN_K_SUB
    st = jnp.dot(k_ref[k0:k0 + ATTN_K_SUB, :], qh_ref[idx], preferred_element_type=F32)
    if masked:
        key = k0 + lax.broadcasted_iota(jnp.int32, st.shape, 0)
        qi = j * ATTN_Q_SUB + lax.broadcasted_iota(jnp.int32, st.shape, 1)
        st = jnp.where(key <= qi, st, -jnp.inf)
    m_old = m_ref[idx]
    m_new = jnp.maximum(m_old, jnp.max(st, axis=0, keepdims=True))
    alpha = jnp.exp2(m_old - m_new)
    pt = jnp.exp2(st - m_new)
    l_ref[idx] = alpha * l_ref[idx] + jnp.sum(pt, axis=0, keepdims=True)
    vt = vt_ref[k0 // tile, :, k0 % tile:k0 % tile + ATTN_K_SUB]
    acc_ref[idx] = alpha * acc_ref[idx] + jnp.dot(vt, pt.astype(BF16), preferred_element_type=F32)
    m_ref[idx] = m_new


def _attn_finish(lam, ga_ref, sub_ref, o_ref, l_ref, acc_ref, lam_init):
    n_q = acc_ref.shape[0] // 2
    for j in range(n_q):
        o = (acc_ref[j] / l_ref[j] - lam * (acc_ref[n_q + j] / l_ref[n_q + j])).T
        rows = slice(j * ATTN_Q_SUB, (j + 1) * ATTN_Q_SUB)
        y = _rms(o, sub_ref[...]) * (1.0 - lam_init)
        o_ref[rows, :] = (y * _silu(ga_ref[rows, :])).astype(BF16)


def _mix_out(x, pooled, gp_ref, attn_part, wmix_ref, ps_ref, wout_ref, gpost_ref):
    parts = []
    for g, pg in enumerate(pooled):
        cols = slice(g * POOL_GROUP, (g + 1) * POOL_GROUP)
        mixed = jnp.dot(pg.astype(BF16), wmix_ref[g], preferred_element_type=F32) * ps_ref[:, cols]
        parts.append((mixed * _silu(gp_ref[:, cols])).astype(BF16))
    mixed_all = jnp.concatenate(parts + [attn_part], axis=1)
    z = jnp.dot(mixed_all, wout_ref[...], preferred_element_type=F32)
    return x + _rms(z, gpost_ref[...])


def _finish_kernel(x_ref, u_ref, halo_ref, gp_ref, am_ref, wmix_ref, ps_ref, wout_ref, gpost_ref,
                   y_ref, buf_ref, *, rows, blocks_per_seq):
    blk_in_seq = pl.program_id(0) % blocks_per_seq
    buf_ref[0:HALO, :] = jnp.where(blk_in_seq == 0, 0.0, halo_ref[...])
    buf_ref[HALO:, :] = u_ref[...]
    pos = lax.broadcasted_iota(jnp.int32, (rows, 1), 0) + blk_in_seq * rows
    pooled = []
    for g, w in enumerate(POOL_WINDOWS):
        cols = slice(g * POOL_GROUP, (g + 1) * POOL_GROUP)
        tok = buf_ref[HALO:HALO + rows, cols]
        acc = tok
        for j in range(1, w):
            acc = acc + buf_ref[HALO - j:HALO - j + rows, cols]
        cnt = jnp.minimum(w, pos + 1).astype(F32)
        pooled.append(acc / cnt - tok)
    y_ref[...] = _mix_out(x_ref[...], pooled, gp_ref, am_ref[...], wmix_ref, ps_ref, wout_ref,
                          gpost_ref)


def _prompt_finish(x, u, gp, am, w_mix, pool_scale, w_out, g_post, seq, rows):
    t, d = x.shape
    blocks_per_seq = seq // rows
    halo_per_block = rows // HALO
    row_spec = lambda w: pl.BlockSpec((rows, w), lambda i: (i, 0))
    halo_spec = pl.BlockSpec((HALO, POOL_WIDTH),
                             lambda i: (jnp.maximum(i * halo_per_block - 1, 0), 0))
    const = lambda shape: pl.BlockSpec(shape, lambda i: (0,) * len(shape))
    return pl.pallas_call(
        functools.partial(_finish_kernel, rows=rows, blocks_per_seq=blocks_per_seq),
        grid=(t // rows,),
        in_specs=[row_spec(d), row_spec(POOL_WIDTH), halo_spec, row_spec(POOL_WIDTH),
                  row_spec(ATTN_WIDTH), const(w_mix.shape), const((1, POOL_WIDTH)),
                  const(w_out.shape), const((1, d))],
        out_specs=row_spec(d),
        out_shape=jax.ShapeDtypeStruct((t, d), F32),
        scratch_shapes=[pltpu.VMEM((HALO + rows, POOL_WIDTH), F32)],
        compiler_params=pltpu.CompilerParams(dimension_semantics=("parallel",),
                                             vmem_limit_bytes=VMEM_LIMIT),
        name="prompt_finish",
    )(x, u, u, gp, am, w_mix, pool_scale.reshape(1, POOL_WIDTH), w_out, g_post.reshape(1, d))


def _sample_finish_kernel(x_ref, r_ref, gp_ref, ao_ref, ga_ref, sub_ref, wmix_ref, ps_ref,
                          wout_ref, gpost_ref, y_ref, *, n_new, counts, lam_init):
    pooled = []
    for g, w in enumerate(POOL_WINDOWS):
        cols = slice(g * POOL_GROUP, (g + 1) * POOL_GROUP)
        per_step = []
        for i in range(n_new):
            tok = r_ref[POOL_BUF + i, :, cols]
            acc = tok
            for j in range(1, w):
                acc = acc + r_ref[POOL_BUF + i - j, :, cols]
            per_step.append(acc / counts[g][i] - tok)
        pooled.append(jnp.concatenate(per_step, axis=0))
    heads = []
    for h in range(N_HEADS):
        cols = slice(h * V_DIM, (h + 1) * V_DIM)
        y = _rms(ao_ref[:, cols], sub_ref[...]) * (1.0 - lam_init)
        heads.append((y * _silu(ga_ref[:, cols])).astype(BF16))
    attn_part = jnp.concatenate(heads, axis=1)
    y_ref[...] = _mix_out(x_ref[...], pooled, gp_ref, attn_part, wmix_ref, ps_ref, wout_ref,
                          gpost_ref)


def _sample_finish(x, rows_hist, gp, ao, ga, subln, w_mix, pool_scale, w_out, g_post,
                   n_new, past_len, lam_init):
    t, d = x.shape
    counts = tuple(tuple(float(min(w, past_len + i + 1)) for i in range(n_new))
                   for w in POOL_WINDOWS)
    return pl.pallas_call(
        functools.partial(_sample_finish_kernel, n_new=n_new, counts=counts, lam_init=lam_init),
        out_shape=jax.ShapeDtypeStruct((t, d), F32),
        compiler_params=pltpu.CompilerParams(vmem_limit_bytes=VMEM_LIMIT),
        name="sample_finish",
    )(x, rows_hist, gp, ao, ga, subln.reshape(1, V_DIM), w_mix,
      pool_scale.reshape(1, POOL_WIDTH), w_out, g_post.reshape(1, d))


def _decode_step(t, total, steps_per_batch, lam, pt_ref, q_ref, kn_ref, vn_ref, ck_hbm, cv_hbm,
                 o_ref, kbuf, vbuf, sem, wq_ref, m_ref, l_ref, acc_ref, pad_ref,
                 *, n_new, pages, layer):
    j = t % steps_per_batch

    def copies(step, slot):
        out = []
        for r in range(pages):
            page = pt_ref[step * pages + r]
            out.append(pltpu.make_async_copy(ck_hbm.at[layer, page], kbuf.at[slot, r], sem.at[0, slot]))
            out.append(pltpu.make_async_copy(cv_hbm.at[layer, page], vbuf.at[slot, r], sem.at[1, slot]))
        return out

    @pl.when(t == 0)
    def _():
        for step in range(DECODE_SLOTS - 1):
            for c in copies(step, step):
                c.start()

    ahead = t + (DECODE_SLOTS - 1)

    @pl.when(ahead < total)
    def _():
        for c in copies(ahead, ahead % DECODE_SLOTS):
            c.start()

    slot = t % DECODE_SLOTS
    for c in copies(t, slot):
        c.wait()
    k_refs = [kbuf.at[slot, r] for r in range(pages)]
    v_refs = [vbuf.at[slot, r] for r in range(pages)]
    qh = n_new * N_HEADS
    page_rows = PAGE_SIZE * N_HEADS
    nt = (((1,), (1,)), ((), ()))
    row = lax.broadcasted_iota(jnp.int32, (2 * qh, page_rows), 0)
    col = lax.broadcasted_iota(jnp.int32, (2 * qh, page_rows), 1)
    same_head = (row % N_HEADS) == (col % N_HEADS)

    def update(state, scores, vals):
        m, l, acc = state
        m_new = m
        for s in scores:
            m_new = jnp.maximum(m_new, jnp.max(s, axis=-1, keepdims=True))
        alpha = jnp.exp2(m - m_new)
        l = alpha * l
        acc = alpha * acc
        for s, v in zip(scores, vals):
            p = jnp.exp2(s - m_new)
            l = l + jnp.sum(p, axis=-1, keepdims=True)
            acc = acc + jnp.dot(p.astype(BF16), v, preferred_element_type=F32)
        return m_new, l, acc

    def load_state():
        return m_ref[...], l_ref[...], acc_ref[...]

    def store_state(state):
        m_ref[...], l_ref[...], acc_ref[...] = state

    @pl.when(j == 0)
    def _():
        q = q_ref[...]
        lane = lax.broadcasted_iota(jnp.int32, q.shape, 1)
        wq_ref[...] = jnp.concatenate([jnp.where(lane < HEAD_DIM, q, 0.0),
                                       jnp.where(lane >= HEAD_DIM, q, 0.0)], axis=0).astype(BF16)
        pad_ref[...] = jnp.zeros(pad_ref.shape, F32)
        pad_ref[0:qh, :] = kn_ref[...]
        s = lax.dot_general(wq_ref[...], pad_ref[...].astype(BF16), nt, preferred_element_type=F32)
        causal = (col // N_HEADS) <= (row % qh) // N_HEADS
        s = jnp.where(same_head, jnp.where(causal, s, -jnp.inf), -jnp.inf)
        pad_ref[0:qh, :] = vn_ref[...]
        empty = (jnp.full(m_ref.shape, -jnp.inf, F32), jnp.zeros(l_ref.shape, F32),
                 jnp.zeros(acc_ref.shape, F32))
        store_state(update(empty, [s], [pad_ref[...].astype(BF16)]))

    wq = wq_ref[...]
    scores = [jnp.where(same_head,
                        lax.dot_general(wq, kr[...].astype(BF16), nt, preferred_element_type=F32),
                        -jnp.inf) for kr in k_refs]
    store_state(update(load_state(), scores, [vr[...].astype(BF16) for vr in v_refs]))

    @pl.when(j == steps_per_batch - 1)
    def _():
        n = acc_ref[...] / l_ref[...]
        o_ref[...] = n[:qh] - lam * n[qh:]


def _attention_kernel(pt_ref, q_ref, kn_ref, vn_ref, ck_hbm, cv_hbm, qt_ref, k_ref, vt_ref, ga_ref,
                      sub_ref, lq1_ref, lk1_ref, lq2_ref, lk2_ref, dec_ref, att_ref,
                      kbuf, vbuf, sem, wq_ref, dm_ref, dl_ref, dacc_ref, pad_ref,
                      am_ref, al_ref, aacc_ref, qh_ref,
                      *, n_new, pages, layer, lam_init, steps_per_batch, phases):
    phase = pl.program_id(1)
    n_phases = pl.num_programs(1)
    t = pl.program_id(0) * n_phases + phase
    lam = _lam(lq1_ref, lk1_ref, lq2_ref, lk2_ref, lam_init)
    _decode_step(t, pl.num_programs(0) * n_phases, steps_per_batch, lam, pt_ref, q_ref, kn_ref,
                 vn_ref, ck_hbm, cv_hbm, dec_ref, kbuf, vbuf, sem, wq_ref, dm_ref, dl_ref,
                 dacc_ref, pad_ref, n_new=n_new, pages=pages, layer=layer)

    busy = [p for p, tiles in enumerate(phases) if tiles]
    for p in busy:
        @pl.when(phase == p)
        def _(p=p):
            if p == busy[0]:
                _attn_init(qt_ref, am_ref, al_ref, aacc_ref, qh_ref)
            for tile in phases[p]:
                _attn_tile(*tile, k_ref, vt_ref, am_ref, al_ref, aacc_ref, qh_ref)
            if p == busy[-1]:
                _attn_finish(lam, ga_ref, sub_ref, att_ref, al_ref, aacc_ref, lam_init)


def _attention(q, k_new, v_new, cache_k, cache_v, page_table, qt, kb, vt, ga, subln, lams,
               layer, lam_init, batch, seq):
    db, qh, width = q.shape
    t = kb.shape[0]
    tile = qt.shape[2]
    pages = DECODE_PAGES
    page_rows = cache_k.shape[2]
    steps_per_batch = page_table.shape[1] // pages
    n_seqs = batch * N_HEADS
    n_phases = db * steps_per_batch // n_seqs
    assert n_phases * n_seqs == db * steps_per_batch and n_phases * n_seqs >= DECODE_SLOTS - 1
    stride = n_phases // ATTN_RUNS
    assert stride * ATTN_RUNS == n_phases
    runs = _attn_phases(seq, ATTN_RUNS)
    phases = tuple(runs[p // stride] if p % stride == 0 else () for p in range(n_phases))
    n_state = 2 * (seq // ATTN_Q_SUB)

    tok_spec = pl.BlockSpec((None, qh, width),
                            lambda g, p, pt: ((g * n_phases + p) // steps_per_batch, 0, 0))
    hbm_spec = pl.BlockSpec(memory_space=pl.ANY)
    seq_spec = pl.BlockSpec((seq, V_DIM), lambda g, p, pt: (g // N_HEADS, g % N_HEADS))
    t_spec = pl.BlockSpec((seq // tile, V_DIM, tile),
                          lambda g, p, pt: (g // N_HEADS, g % N_HEADS, 0))
    vec = lambda n: pl.BlockSpec((1, n), lambda g, p, pt: (0, 0))
    ring = pltpu.VMEM((DECODE_SLOTS, pages, page_rows, width), F32)
    grid_spec = pltpu.PrefetchScalarGridSpec(
        num_scalar_prefetch=1,
        grid=(n_seqs, n_phases),
        in_specs=([tok_spec] * 3 + [hbm_spec] * 2 + [t_spec, seq_spec, t_spec, seq_spec,
                                                     vec(V_DIM)] + [vec(HEAD_DIM)] * 4),
        out_specs=[tok_spec, seq_spec],
        scratch_shapes=[ring, ring, pltpu.SemaphoreType.DMA((2, DECODE_SLOTS)),
                        pltpu.VMEM((2 * qh, width), BF16),
                        pltpu.VMEM((2 * qh, 1), F32),
                        pltpu.VMEM((2 * qh, 1), F32),
                        pltpu.VMEM((2 * qh, width), F32),
                        pltpu.VMEM((page_rows, width), F32),
                        pltpu.VMEM((n_state, 1, ATTN_Q_SUB), F32),
                        pltpu.VMEM((n_state, 1, ATTN_Q_SUB), F32),
                        pltpu.VMEM((n_state, V_DIM, ATTN_Q_SUB), F32),
                        pltpu.VMEM((n_state, V_DIM, ATTN_Q_SUB), BF16)],
    )
    return pl.pallas_call(
        functools.partial(_attention_kernel, n_new=qh // N_HEADS, pages=pages, layer=layer,
                          lam_init=lam_init, steps_per_batch=steps_per_batch, phases=phases),
        grid_spec=grid_spec,
        out_shape=[jax.ShapeDtypeStruct((db, qh, width), F32),
                   jax.ShapeDtypeStruct((t, ATTN_WIDTH), BF16)],
        compiler_params=pltpu.CompilerParams(dimension_semantics=("arbitrary", "arbitrary"),
                                             vmem_limit_bytes=VMEM_LIMIT),
        name="attention",
    )(page_table.reshape(-1), q, k_new, v_new, cache_k, cache_v, qt, kb, vt, ga,
      subln.reshape(1, V_DIM), *lams)


def kernel(x_prompt, x_sample, cache_k, cache_v, state_pool, page_table, norm_pre, norm_post,
           w_in, w_pool_mix, pool_scale, lambda_q1, lambda_k1, lambda_q2, lambda_k2, subln, w_out):
    batch, seq, d = x_prompt.shape
    db, n_new, _ = x_sample.shape
    depth = w_in.shape[0]
    n_phys = cache_k.shape[1]
    past_len = page_table.shape[1] * PAGE_SIZE
    assert seq % PROJ_ROWS == 0 and seq % ATTN_Q_SUB == 0 and page_table.shape[1] % DECODE_PAGES == 0
    assert w_in.shape[2] == N_PROJ * ATTN_WIDTH and POOL_WIDTH == ATTN_WIDTH

    w_in_b = w_in.astype(BF16)
    w_out_b = w_out.astype(BF16)
    w_mix_b = w_pool_mix.astype(BF16)
    cache_k = cache_k.reshape(depth, n_phys, PAGE_SIZE * N_HEADS, V_DIM)
    cache_v = cache_v.reshape(depth, n_phys, PAGE_SIZE * N_HEADS, V_DIM)
    tables_p = _rope_tables(jnp.arange(seq))
    tables_s = _rope_tables(past_len + jnp.repeat(jnp.arange(n_new), db))
    xp = x_prompt.reshape(batch * seq, d)
    xs = x_sample.transpose(1, 0, 2).reshape(n_new * db, d)

    def to_batch_major(a):
        return (a.reshape(n_new, db, N_HEADS, V_DIM).transpose(1, 0, 2, 3)
                .reshape(db, n_new * N_HEADS, V_DIM))

    outs = [[] for _ in range(4)]
    prev_k, prev_v = [], []
    for layer in range(depth):
        lam_init = _lam_init(layer)
        lams = [v[layer].reshape(1, HEAD_DIM) for v in (lambda_q1, lambda_k1, lambda_q2, lambda_k2)]
        prev_kv = prev_k + prev_v if layer == depth - 1 else []
        up, gpp, kp, vp, gap, qt, kb, vt = _project(xp, norm_pre[layer], w_in_b[layer], tables_p,
                                                    PROJ_ROWS, True, prev_kv)
        u, gp, k, v, ga, q = _project(xs, norm_pre[layer], w_in_b[layer], tables_s, n_new * db, False)
        ao, am = _attention(to_batch_major(q), to_batch_major(k), to_batch_major(v), cache_k,
                            cache_v, page_table, qt, kb, vt, gap, subln[layer], lams, layer,
                            lam_init, batch, seq)
        xp = _prompt_finish(xp, up, gpp, am, w_mix_b[layer], pool_scale[layer], w_out_b[layer],
                            norm_post[layer], seq, PROJ_ROWS)
        if layer == depth - 1:
            new_k_prompt = kp.reshape(depth, batch, seq, N_HEADS, V_DIM)
            new_v_prompt = vp.reshape(depth, batch, seq, N_HEADS, V_DIM)
        else:
            prev_k.append(kp[0])
            prev_v.append(vp[0])
        outs[0].append(up.reshape(batch, seq, POOL_WIDTH)[:, seq - POOL_BUF:])
        hist = jnp.concatenate([state_pool[layer].transpose(1, 0, 2),
                                u.reshape(n_new, db, POOL_WIDTH)], axis=0)
        ao = ao.reshape(db, n_new, ATTN_WIDTH).transpose(1, 0, 2).reshape(n_new * db, ATTN_WIDTH)
        xs = _sample_finish(xs, hist, gp, ao, ga,
                            subln[layer], w_mix_b[layer], pool_scale[layer], w_out_b[layer],
                            norm_post[layer], n_new, past_len, lam_init)
        outs[1].append(to_batch_major(k).reshape(db, n_new, N_HEADS, V_DIM))
        outs[2].append(to_batch_major(v).reshape(db, n_new, N_HEADS, V_DIM))
        outs[3].append(hist[n_new:].transpose(1, 0, 2))
    y_prompt = xp.reshape(batch, seq, d)
    y_sample = xs.reshape(n_new, db, d).transpose(1, 0, 2)
    new_pool_prompt, new_k_sample, new_v_sample, new_pool_sample = (jnp.stack(o) for o in outs)
    return (y_prompt, y_sample, new_k_prompt, new_v_prompt, new_pool_prompt,
            new_k_sample, new_v_sample, new_pool_sample)
```

```python
import functools
import math

import jax
import jax.numpy as jnp
from jax import lax
from jax.experimental import pallas as pl
from jax.experimental.pallas import tpu as pltpu

F32 = jnp.float32
BF16 = jnp.bfloat16

POOL_WINDOWS = (2, 4, 8, 16)
POOL_GROUP = 128
POOL_WIDTH = POOL_GROUP * len(POOL_WINDOWS)
POOL_BUF = max(POOL_WINDOWS) - 1
HEAD_DIM = 64
V_DIM = 2 * HEAD_DIM
N_HEADS = 4
ATTN_WIDTH = N_HEADS * V_DIM
ROT_DIM = HEAD_DIM // 4
ROPE_THETA = 500000.0
EPS = 1e-6
PAGE_SIZE = 128
N_PROJ = 6
Q_SCALE = HEAD_DIM ** -0.5 * math.log2(math.e)

LANES = 128
SUBLANES = 8
HALO = 2 * SUBLANES
VMEM_LIMIT = 56 * 1024 * 1024

PROJ_ROWS = 512
FINISH_ROWS = 1024
ATTN_Q_SUB = 256
ATTN_K_SUB = 128
ATTN_MASKED_TILE_COST = 1.2
ATTN_INIT_COST = 3.0
ATTN_FINISH_COST = 11.0
DECODE_PAGES = 16
DECODE_SLOTS = 4
ATTN_RUNS = 2


def _lam_init(layer):
    return 0.8 - 0.6 * math.exp(-0.3 * layer)


def _lam(lq1_ref, lk1_ref, lq2_ref, lk2_ref, lam_init):
    a = jnp.sum(lq1_ref[...] * lk1_ref[...], axis=-1, keepdims=True)
    b = jnp.sum(lq2_ref[...] * lk2_ref[...], axis=-1, keepdims=True)
    return jnp.exp(a) - jnp.exp(b) + lam_init


def _silu(x):
    return x * jax.nn.sigmoid(x)


def _rms(x, g):
    return x * lax.rsqrt(jnp.mean(x * x, axis=-1, keepdims=True) + EPS) * g


def _rope_tables(pos):
    half = ROT_DIM // 2
    inv = ROPE_THETA ** (-(jnp.arange(half, dtype=F32) * 2.0) / ROT_DIM)
    ang = pos.astype(F32)[:, None] * inv[None, :]
    cos, sin = jnp.cos(ang), jnp.sin(ang)
    n = pos.shape[0]
    rest = HEAD_DIM - ROT_DIM
    zeros_h = jnp.zeros((n, half), F32)
    c = jnp.concatenate([cos, cos, jnp.ones((n, rest), F32)], axis=1)
    a = jnp.concatenate([-sin, zeros_h, jnp.zeros((n, rest), F32)], axis=1)
    b = jnp.concatenate([zeros_h, sin, jnp.zeros((n, rest), F32)], axis=1)
    rep = LANES // HEAD_DIM
    return tuple(jnp.tile(t, (1, rep)) for t in (c, a, b))


def _proj_kernel(x_ref, g_ref, w_ref, c_ref, a_ref, b_ref, *refs, for_prompt, n_prev):
    prev_k, prev_v = refs[:n_prev], refs[n_prev:2 * n_prev]
    u_ref, gp_ref, k_ref, v_ref, ga_ref, *q_refs = refs[2 * n_prev:]
    rows = x_ref.shape[0]
    xb = _rms(x_ref[...], g_ref[...]).astype(BF16)
    width = ATTN_WIDTH
    half = ROT_DIM // 2

    def mm(j):
        return jnp.dot(xb, w_ref[:, j * width:(j + 1) * width], preferred_element_type=F32)

    def rope(z, j):
        zc = z[:, j * LANES:(j + 1) * LANES]
        return (zc * c_ref[...] + pltpu.roll(zc, LANES - half, 1) * a_ref[...]
                + pltpu.roll(zc, half, 1) * b_ref[...])

    u_ref[...] = mm(0)
    gp_ref[...] = mm(1)
    q = mm(2)
    k = mm(3)
    v = mm(4)
    for p in range(n_prev):
        k_ref[p] = prev_k[p][...]
        v_ref[p] = prev_v[p][...]
    for j in range(N_HEADS):
        cols = slice(j * LANES, (j + 1) * LANES)
        qr = rope(q, j) * Q_SCALE
        kr = rope(k, j)
        if for_prompt:
            qt_ref, kb_ref, vt_ref = q_refs
            head_rows = pl.ds(j, rows, stride=N_HEADS)
            k_ref[n_prev, head_rows, :] = kr
            v_ref[n_prev, head_rows, :] = v[:, cols]
            qt_ref[cols, :] = qr.T.astype(BF16)
            kb_ref[:, cols] = kr.astype(BF16)
            vt_ref[cols, :] = v[:, cols].T.astype(BF16)
        else:
            k_ref[:, cols] = kr
            v_ref[:, cols] = v[:, cols]
            q_refs[0][:, cols] = qr
    ga_ref[...] = mm(5)


def _project(x, g_pre, w_in, tables, rows, for_prompt, prev_kv=()):
    t, d = x.shape
    n_blocks = t // rows
    n_tab = tables[0].shape[0] // rows
    n_prev = len(prev_kv) // 2
    width = ATTN_WIDTH
    row_spec = lambda w: pl.BlockSpec((rows, w), lambda i: (i, 0))
    tab_spec = pl.BlockSpec((rows, LANES), lambda i: (i % n_tab, 0))
    f32_out = jax.ShapeDtypeStruct((t, width), F32)
    if for_prompt:
        kv_spec = pl.BlockSpec((n_prev + 1, rows * N_HEADS, V_DIM), lambda i: (0, i, 0))
        kv_out = jax.ShapeDtypeStruct((n_prev + 1, t * N_HEADS, V_DIM), F32)
        t_spec = pl.BlockSpec((None, width, rows), lambda i: (i, 0, 0))
        t_out = jax.ShapeDtypeStruct((n_blocks, width, rows), BF16)
        out_specs = [row_spec(width)] * 2 + [kv_spec] * 2 + [row_spec(width), t_spec,
                                                              row_spec(width), t_spec]
        out_shape = [f32_out] * 2 + [kv_out] * 2 + [f32_out, t_out,
                                                    jax.ShapeDtypeStruct((t, width), BF16), t_out]
    else:
        out_specs = [row_spec(width)] * 6
        out_shape = [f32_out] * 6
    prev_spec = pl.BlockSpec((rows * N_HEADS, V_DIM), lambda i: (i, 0))
    return pl.pallas_call(
        functools.partial(_proj_kernel, for_prompt=for_prompt, n_prev=n_prev),
        grid=(n_blocks,),
        in_specs=[row_spec(d), pl.BlockSpec((1, d), lambda i: (0, 0)),
                  pl.BlockSpec((d, N_PROJ * width), lambda i: (0, 0),
                               pipeline_mode=pl.Buffered(1)),
                  tab_spec, tab_spec, tab_spec] + [prev_spec] * (2 * n_prev),
        out_specs=out_specs,
        out_shape=out_shape,
        compiler_params=pltpu.CompilerParams(dimension_semantics=("parallel",),
                                             vmem_limit_bytes=VMEM_LIMIT),
        name="proj",
    )(x, g_pre.reshape(1, d), w_in, *tables, *prev_kv)


def _attn_tiles(seq):
    tiles = []
    for kk in range(seq // ATTN_K_SUB):
        for half in range(2):
            for j in range(seq // ATTN_Q_SUB):
                first_key, last_key = kk * ATTN_K_SUB, (kk + 1) * ATTN_K_SUB - 1
                first_q, last_q = j * ATTN_Q_SUB, (j + 1) * ATTN_Q_SUB - 1
                if first_key <= last_q:
                    tiles.append((half, j, kk, last_key > first_q))
    return tiles


def _attn_phases(seq, n_phases):
    tiles = _attn_tiles(seq)
    cost = [ATTN_MASKED_TILE_COST if t[3] else 1.0 for t in tiles]
    budget = (ATTN_INIT_COST + sum(cost) + ATTN_FINISH_COST) / n_phases
    phases = [[] for _ in range(n_phases)]
    spent = ATTN_INIT_COST
    for t, c in zip(tiles, cost):
        phases[min(int((spent + c / 2) / budget), n_phases - 1)].append(t)
        spent += c
    return tuple(tuple(p) for p in phases)


def _attn_init(qt_ref, m_ref, l_ref, acc_ref, qh_ref):
    tile = qt_ref.shape[2]
    n_q = qh_ref.shape[0] // 2
    m_ref[...] = jnp.full(m_ref.shape, -jnp.inf, F32)
    l_ref[...] = jnp.zeros(l_ref.shape, F32)
    acc_ref[...] = jnp.zeros(acc_ref.shape, F32)
    feat = lax.broadcasted_iota(jnp.int32, (V_DIM, ATTN_Q_SUB), 0)
    for j in range(n_q):
        q0 = j * ATTN_Q_SUB
        qt = qt_ref[q0 // tile, :, q0 % tile:q0 % tile + ATTN_Q_SUB]
        qh_ref[j] = jnp.where(feat < HEAD_DIM, qt, jnp.zeros_like(qt))
        qh_ref[n_q + j] = jnp.where(feat >= HEAD_DIM, qt, jnp.zeros_like(qt))


def _attn_tile(half, j, kk, masked, k_ref, vt_ref, m_ref, l_ref, acc_ref, qh_ref):
    tile = vt_ref.shape[2]
    idx = half * (qh_ref.shape[0] // 2) + j
    k0 = kk * ATTN_K_SUB
    st = jnp.dot(k_ref[k0:k0 + ATTN_K_SUB, :], qh_ref[idx], preferred_element_type=F32)
    if masked:
        key = k0 + lax.broadcasted_iota(jnp.int32, st.shape, 0)
        qi = j * ATTN_Q_SUB + lax.broadcasted_iota(jnp.int32, st.shape, 1)
        st = jnp.where(key <= qi, st, -jnp.inf)
    m_old = m_ref[idx]
    m_new = jnp.maximum(m_old, jnp.max(st, axis=0, keepdims=True))
    alpha = jnp.exp2(m_old - m_new)
    pt = jnp.exp2(st - m_new)
    l_ref[idx] = alpha * l_ref[idx] + jnp.sum(pt, axis=0, keepdims=True)
    vt = vt_ref[k0 // tile, :, k0 % tile:k0 % tile + ATTN_K_SUB]
    acc_ref[idx] = alpha * acc_ref[idx] + jnp.dot(vt, pt.astype(BF16), preferred_element_type=F32)
    m_ref[idx] = m_new


def _attn_finish(lam, ga_ref, sub_ref, o_ref, l_ref, acc_ref, lam_init):
    n_q = acc_ref.shape[0] // 2
    for j in range(n_q):
        o = (acc_ref[j] / l_ref[j] - lam * (acc_ref[n_q + j] / l_ref[n_q + j])).T
        rows = slice(j * ATTN_Q_SUB, (j + 1) * ATTN_Q_SUB)
        y = _rms(o, sub_ref[...]) * (1.0 - lam_init)
        o_ref[rows, :] = (y * _silu(ga_ref[rows, :])).astype(BF16)


def _mix_out(x, pooled, gp_ref, attn_part, wmix_ref, ps_ref, wout_ref, gpost_ref):
    parts = []
    for g, pg in enumerate(pooled):
        cols = slice(g * POOL_GROUP, (g + 1) * POOL_GROUP)
        mixed = jnp.dot(pg.astype(BF16), wmix_ref[g], preferred_element_type=F32) * ps_ref[:, cols]
        parts.append((mixed * _silu(gp_ref[:, cols])).astype(BF16))
    mixed_all = jnp.concatenate(parts + [attn_part], axis=1)
    z = jnp.dot(mixed_all, wout_ref[...], preferred_element_type=F32)
    return x + _rms(z, gpost_ref[...])


def _window_sum(buf_ref, cols, w, rows):
    if w >= 2 * SUBLANES and w % (2 * SUBLANES) == 0:
        half = w // 2
        assert half + (half - 1) <= HALO
        part = buf_ref[HALO - half:HALO + rows, cols]
        for j in range(1, half):
            part = part + buf_ref[HALO - half - j:HALO + rows - j, cols]
        return part[half:] + part[:rows]
    acc = buf_ref[HALO:HALO + rows, cols]
    for j in range(1, w):
        acc = acc + buf_ref[HALO - j:HALO - j + rows, cols]
    return acc


def _finish_kernel(x_ref, u_ref, halo_ref, gp_ref, am_ref, wmix_ref, ps_ref, wout_ref, gpost_ref,
                   y_ref, buf_ref, *, rows, blocks_per_seq):
    blk_in_seq = pl.program_id(0) % blocks_per_seq
    buf_ref[0:HALO, :] = jnp.where(blk_in_seq == 0, 0.0, halo_ref[...])
    buf_ref[HALO:, :] = u_ref[...]
    pos = lax.broadcasted_iota(jnp.int32, (rows, 1), 0) + blk_in_seq * rows
    pooled = []
    for g, w in enumerate(POOL_WINDOWS):
        cols = slice(g * POOL_GROUP, (g + 1) * POOL_GROUP)
        tok = buf_ref[HALO:HALO + rows, cols]
        acc = _window_sum(buf_ref, cols, w, rows)
        cnt = jnp.minimum(w, pos + 1).astype(F32)
        pooled.append(acc / cnt - tok)
    y_ref[...] = _mix_out(x_ref[...], pooled, gp_ref, am_ref[...], wmix_ref, ps_ref, wout_ref,
                          gpost_ref)


def _prompt_finish(x, u, gp, am, w_mix, pool_scale, w_out, g_post, seq, rows):
    t, d = x.shape
    blocks_per_seq = seq // rows
    halo_per_block = rows // HALO
    row_spec = lambda w: pl.BlockSpec((rows, w), lambda i: (i, 0))
    halo_spec = pl.BlockSpec((HALO, POOL_WIDTH),
                             lambda i: (jnp.maximum(i * halo_per_block - 1, 0), 0))
    const = lambda shape: pl.BlockSpec(shape, lambda i: (0,) * len(shape))
    return pl.pallas_call(
        functools.partial(_finish_kernel, rows=rows, blocks_per_seq=blocks_per_seq),
        grid=(t // rows,),
        in_specs=[row_spec(d), row_spec(POOL_WIDTH), halo_spec, row_spec(POOL_WIDTH),
                  row_spec(ATTN_WIDTH), const(w_mix.shape), const((1, POOL_WIDTH)),
                  const(w_out.shape), const((1, d))],
        out_specs=row_spec(d),
        out_shape=jax.ShapeDtypeStruct((t, d), F32),
        scratch_shapes=[pltpu.VMEM((HALO + rows, POOL_WIDTH), F32)],
        compiler_params=pltpu.CompilerParams(dimension_semantics=("parallel",),
                                             vmem_limit_bytes=VMEM_LIMIT),
        name="prompt_finish",
    )(x, u, u, gp, am, w_mix, pool_scale.reshape(1, POOL_WIDTH), w_out, g_post.reshape(1, d))


def _sample_finish_kernel(x_ref, r_ref, gp_ref, ao_ref, ga_ref, sub_ref, wmix_ref, ps_ref,
                          wout_ref, gpost_ref, y_ref, *, n_new, counts, lam_init):
    pooled = []
    for g, w in enumerate(POOL_WINDOWS):
        cols = slice(g * POOL_GROUP, (g + 1) * POOL_GROUP)
        per_step = []
        for i in range(n_new):
            tok = r_ref[POOL_BUF + i, :, cols]
            acc = tok
            for j in range(1, w):
                acc = acc + r_ref[POOL_BUF + i - j, :, cols]
            per_step.append(acc / counts[g][i] - tok)
        pooled.append(jnp.concatenate(per_step, axis=0))
    heads = []
    for h in range(N_HEADS):
        cols = slice(h * V_DIM, (h + 1) * V_DIM)
        y = _rms(ao_ref[:, cols], sub_ref[...]) * (1.0 - lam_init)
        heads.append((y * _silu(ga_ref[:, cols])).astype(BF16))
    attn_part = jnp.concatenate(heads, axis=1)
    y_ref[...] = _mix_out(x_ref[...], pooled, gp_ref, attn_part, wmix_ref, ps_ref, wout_ref,
                          gpost_ref)


def _sample_finish(x, rows_hist, gp, ao, ga, subln, w_mix, pool_scale, w_out, g_post,
                   n_new, past_len, lam_init):
    t, d = x.shape
    counts = tuple(tuple(float(min(w, past_len + i + 1)) for i in range(n_new))
                   for w in POOL_WINDOWS)
    return pl.pallas_call(
        functools.partial(_sample_finish_kernel, n_new=n_new, counts=counts, lam_init=lam_init),
        out_shape=jax.ShapeDtypeStruct((t, d), F32),
        compiler_params=pltpu.CompilerParams(vmem_limit_bytes=VMEM_LIMIT),
        name="sample_finish",
    )(x, rows_hist, gp, ao, ga, subln.reshape(1, V_DIM), w_mix,
      pool_scale.reshape(1, POOL_WIDTH), w_out, g_post.reshape(1, d))


def _decode_step(t, total, steps_per_batch, lam, pt_ref, q_ref, kn_ref, vn_ref, ck_hbm, cv_hbm,
                 o_ref, kbuf, vbuf, sem, wq_ref, m_ref, l_ref, acc_ref, pad_ref,
                 *, n_new, pages, layer):
    j = t % steps_per_batch

    def copies(step, slot):
        out = []
        for r in range(pages):
            page = pt_ref[step * pages + r]
            out.append(pltpu.make_async_copy(ck_hbm.at[layer, page], kbuf.at[slot, r], sem.at[0, slot]))
            out.append(pltpu.make_async_copy(cv_hbm.at[layer, page], vbuf.at[slot, r], sem.at[1, slot]))
        return out

    @pl.when(t == 0)
    def _():
        for step in range(DECODE_SLOTS - 1):
            for c in copies(step, step):
                c.start()

    ahead = t + (DECODE_SLOTS - 1)

    @pl.when(ahead < total)
    def _():
        for c in copies(ahead, ahead % DECODE_SLOTS):
            c.start()

    slot = t % DECODE_SLOTS
    for c in copies(t, slot):
        c.wait()
    k_refs = [kbuf.at[slot, r] for r in range(pages)]
    v_refs = [vbuf.at[slot, r] for r in range(pages)]
    qh = n_new * N_HEADS
    page_rows = PAGE_SIZE * N_HEADS
    nt = (((1,), (1,)), ((), ()))
    row = lax.broadcasted_iota(jnp.int32, (2 * qh, page_rows), 0)
    col = lax.broadcasted_iota(jnp.int32, (2 * qh, page_rows), 1)
    same_head = (row % N_HEADS) == (col % N_HEADS)

    def update(state, scores, vals):
        m, l, acc = state
        m_new = m
        for s in scores:
            m_new = jnp.maximum(m_new, jnp.max(s, axis=-1, keepdims=True))
        alpha = jnp.exp2(m - m_new)
        l = alpha * l
        acc = alpha * acc
        for s, v in zip(scores, vals):
            p = jnp.exp2(s - m_new)
            l = l + jnp.sum(p, axis=-1, keepdims=True)
            acc = acc + jnp.dot(p.astype(BF16), v, preferred_element_type=F32)
        return m_new, l, acc

    def load_state():
        return m_ref[...], l_ref[...], acc_ref[...]

    def store_state(state):
        m_ref[...], l_ref[...], acc_ref[...] = state

    @pl.when(j == 0)
    def _():
        q = q_ref[...]
        lane = lax.broadcasted_iota(jnp.int32, q.shape, 1)
        wq_ref[...] = jnp.concatenate([jnp.where(lane < HEAD_DIM, q, 0.0),
                                       jnp.where(lane >= HEAD_DIM, q, 0.0)], axis=0).astype(BF16)
        pad_ref[...] = jnp.zeros(pad_ref.shape, F32)
        pad_ref[0:qh, :] = kn_ref[...]
        s = lax.dot_general(wq_ref[...], pad_ref[...].astype(BF16), nt, preferred_element_type=F32)
        causal = (col // N_HEADS) <= (row % qh) // N_HEADS
        s = jnp.where(same_head, jnp.where(causal, s, -jnp.inf), -jnp.inf)
        pad_ref[0:qh, :] = vn_ref[...]
        empty = (jnp.full(m_ref.shape, -jnp.inf, F32), jnp.zeros(l_ref.shape, F32),
                 jnp.zeros(acc_ref.shape, F32))
        store_state(update(empty, [s], [pad_ref[...].astype(BF16)]))

    wq = wq_ref[...]
    scores = [jnp.where(same_head,
                        lax.dot_general(wq, kr[...].astype(BF16), nt, preferred_element_type=F32),
                        -jnp.inf) for kr in k_refs]
    store_state(update(load_state(), scores, [vr[...].astype(BF16) for vr in v_refs]))

    @pl.when(j == steps_per_batch - 1)
    def _():
        n = acc_ref[...] / l_ref[...]
        o_ref[...] = n[:qh] - lam * n[qh:]


def _attention_kernel(pt_ref, q_ref, kn_ref, vn_ref, ck_hbm, cv_hbm, qt_ref, k_ref, vt_ref, ga_ref,
                      sub_ref, lq1_ref, lk1_ref, lq2_ref, lk2_ref, dec_ref, att_ref,
                      kbuf, vbuf, sem, wq_ref, dm_ref, dl_ref, dacc_ref, pad_ref,
                      am_ref, al_ref, aacc_ref, qh_ref,
                      *, n_new, pages, layer, lam_init, steps_per_batch, phases):
    phase = pl.program_id(1)
    n_phases = pl.num_programs(1)
    t = pl.program_id(0) * n_phases + phase
    lam = _lam(lq1_ref, lk1_ref, lq2_ref, lk2_ref, lam_init)
    _decode_step(t, pl.num_programs(0) * n_phases, steps_per_batch, lam, pt_ref, q_ref, kn_ref,
                 vn_ref, ck_hbm, cv_hbm, dec_ref, kbuf, vbuf, sem, wq_ref, dm_ref, dl_ref,
                 dacc_ref, pad_ref, n_new=n_new, pages=pages, layer=layer)

    busy = [p for p, tiles in enumerate(phases) if tiles]
    for p in busy:
        @pl.when(phase == p)
        def _(p=p):
            if p == busy[0]:
                _attn_init(qt_ref, am_ref, al_ref, aacc_ref, qh_ref)
            for tile in phases[p]:
                _attn_tile(*tile, k_ref, vt_ref, am_ref, al_ref, aacc_ref, qh_ref)
            if p == busy[-1]:
                _attn_finish(lam, ga_ref, sub_ref, att_ref, al_ref, aacc_ref, lam_init)


def _attention(q, k_new, v_new, cache_k, cache_v, page_table, qt, kb, vt, ga, subln, lams,
               layer, lam_init, batch, seq):
    db, qh, width = q.shape
    t = kb.shape[0]
    tile = qt.shape[2]
    pages = DECODE_PAGES
    page_rows = cache_k.shape[2]
    steps_per_batch = page_table.shape[1] // pages
    n_seqs = batch * N_HEADS
    n_phases = db * steps_per_batch // n_seqs
    assert n_phases * n_seqs == db * steps_per_batch and n_phases * n_seqs >= DECODE_SLOTS - 1
    stride = n_phases // ATTN_RUNS
    assert stride * ATTN_RUNS == n_phases
    runs = _attn_phases(seq, ATTN_RUNS)
    phases = tuple(runs[p // stride] if p % stride == 0 else () for p in range(n_phases))
    n_state = 2 * (seq // ATTN_Q_SUB)

    tok_spec = pl.BlockSpec((None, qh, width),
                            lambda g, p, pt: ((g * n_phases + p) // steps_per_batch, 0, 0))
    hbm_spec = pl.BlockSpec(memory_space=pl.ANY)
    seq_spec = pl.BlockSpec((seq, V_DIM), lambda g, p, pt: (g // N_HEADS, g % N_HEADS))
    t_spec = pl.BlockSpec((seq // tile, V_DIM, tile),
                          lambda g, p, pt: (g // N_HEADS, g % N_HEADS, 0))
    vec = lambda n: pl.BlockSpec((1, n), lambda g, p, pt: (0, 0))
    ring = pltpu.VMEM((DECODE_SLOTS, pages, page_rows, width), F32)
    grid_spec = pltpu.PrefetchScalarGridSpec(
        num_scalar_prefetch=1,
        grid=(n_seqs, n_phases),
        in_specs=([tok_spec] * 3 + [hbm_spec] * 2 + [t_spec, seq_spec, t_spec, seq_spec,
                                                     vec(V_DIM)] + [vec(HEAD_DIM)] * 4),
        out_specs=[tok_spec, seq_spec],
        scratch_shapes=[ring, ring, pltpu.SemaphoreType.DMA((2, DECODE_SLOTS)),
                        pltpu.VMEM((2 * qh, width), BF16),
                        pltpu.VMEM((2 * qh, 1), F32),
                        pltpu.VMEM((2 * qh, 1), F32),
                        pltpu.VMEM((2 * qh, width), F32),
                        pltpu.VMEM((page_rows, width), F32),
                        pltpu.VMEM((n_state, 1, ATTN_Q_SUB), F32),
                        pltpu.VMEM((n_state, 1, ATTN_Q_SUB), F32),
                        pltpu.VMEM((n_state, V_DIM, ATTN_Q_SUB), F32),
                        pltpu.VMEM((n_state, V_DIM, ATTN_Q_SUB), BF16)],
    )
    return pl.pallas_call(
        functools.partial(_attention_kernel, n_new=qh // N_HEADS, pages=pages, layer=layer,
                          lam_init=lam_init, steps_per_batch=steps_per_batch, phases=phases),
        grid_spec=grid_spec,
        out_shape=[jax.ShapeDtypeStruct((db, qh, width), F32),
                   jax.ShapeDtypeStruct((t, ATTN_WIDTH), BF16)],
        compiler_params=pltpu.CompilerParams(dimension_semantics=("arbitrary", "arbitrary"),
                                             vmem_limit_bytes=VMEM_LIMIT),
        name="attention",
    )(page_table.reshape(-1), q, k_new, v_new, cache_k, cache_v, qt, kb, vt, ga,
      subln.reshape(1, V_DIM), *lams)


def kernel(x_prompt, x_sample, cache_k, cache_v, state_pool, page_table, norm_pre, norm_post,
           w_in, w_pool_mix, pool_scale, lambda_q1, lambda_k1, lambda_q2, lambda_k2, subln, w_out):
    batch, seq, d = x_prompt.shape
    db, n_new, _ = x_sample.shape
    depth = w_in.shape[0]
    n_phys = cache_k.shape[1]
    past_len = page_table.shape[1] * PAGE_SIZE
    assert seq % PROJ_ROWS == 0 and seq % FINISH_ROWS == 0 and seq % ATTN_Q_SUB == 0 and page_table.shape[1] % DECODE_PAGES == 0
    assert w_in.shape[2] == N_PROJ * ATTN_WIDTH and POOL_WIDTH == ATTN_WIDTH

    w_in_b = w_in.astype(BF16)
    w_out_b = w_out.astype(BF16)
    w_mix_b = w_pool_mix.astype(BF16)
    cache_k = cache_k.reshape(depth, n_phys, PAGE_SIZE * N_HEADS, V_DIM)
    cache_v = cache_v.reshape(depth, n_phys, PAGE_SIZE * N_HEADS, V_DIM)
    tables_p = _rope_tables(jnp.arange(seq))
    tables_s = _rope_tables(past_len + jnp.repeat(jnp.arange(n_new), db))
    xp = x_prompt.reshape(batch * seq, d)
    xs = x_sample.transpose(1, 0, 2).reshape(n_new * db, d)

    def to_batch_major(a):
        return (a.reshape(n_new, db, N_HEADS, V_DIM).transpose(1, 0, 2, 3)
                .reshape(db, n_new * N_HEADS, V_DIM))

    outs = [[] for _ in range(4)]
    prev_k, prev_v = [], []
    for layer in range(depth):
        lam_init = _lam_init(layer)
        lams = [v[layer].reshape(1, HEAD_DIM) for v in (lambda_q1, lambda_k1, lambda_q2, lambda_k2)]
        prev_kv = prev_k + prev_v if layer == depth - 1 else []
        up, gpp, kp, vp, gap, qt, kb, vt = _project(xp, norm_pre[layer], w_in_b[layer], tables_p,
                                                    PROJ_ROWS, True, prev_kv)
        u, gp, k, v, ga, q = _project(xs, norm_pre[layer], w_in_b[layer], tables_s, n_new * db, False)
        ao, am = _attention(to_batch_major(q), to_batch_major(k), to_batch_major(v), cache_k,
                            cache_v, page_table, qt, kb, vt, gap, subln[layer], lams, layer,
                            lam_init, batch, seq)
        xp = _prompt_finish(xp, up, gpp, am, w_mix_b[layer], pool_scale[layer], w_out_b[layer],
                            norm_post[layer], seq, FINISH_ROWS)
        if layer == depth - 1:
            new_k_prompt = kp.reshape(depth, batch, seq, N_HEADS, V_DIM)
            new_v_prompt = vp.reshape(depth, batch, seq, N_HEADS, V_DIM)
        else:
            prev_k.append(kp[0])
            prev_v.append(vp[0])
        outs[0].append(up.reshape(batch, seq, POOL_WIDTH)[:, seq - POOL_BUF:])
        hist = jnp.concatenate([state_pool[layer].transpose(1, 0, 2),
                                u.reshape(n_new, db, POOL_WIDTH)], axis=0)
        ao = ao.reshape(db, n_new, ATTN_WIDTH).transpose(1, 0, 2).reshape(n_new * db, ATTN_WIDTH)
        xs = _sample_finish(xs, hist, gp, ao, ga,
                            subln[layer], w_mix_b[layer], pool_scale[layer], w_out_b[layer],
                            norm_post[layer], n_new, past_len, lam_init)
        outs[1].append(to_batch_major(k).reshape(db, n_new, N_HEADS, V_DIM))
        outs[2].append(to_batch_major(v).reshape(db, n_new, N_HEADS, V_DIM))
        outs[3].append(hist[n_new:].transpose(1, 0, 2))
    y_prompt = xp.reshape(batch, seq, d)
    y_sample = xs.reshape(n_new, db, d).transpose(1, 0, 2)
    new_pool_prompt, new_k_sample, new_v_sample, new_pool_sample = (jnp.stack(o) for o in outs)
    return (y_prompt, y_sample, new_k_prompt, new_v_prompt, new_pool_prompt,
            new_k_sample, new_v_sample, new_pool_sample)
```

```python
import functools
import math

import jax
import jax.numpy as jnp
from jax import lax
from jax.experimental import pallas as pl
from jax.experimental.pallas import tpu as pltpu

F32 = jnp.float32
BF16 = jnp.bfloat16

POOL_WINDOWS = (2, 4, 8, 16)
POOL_GROUP = 128
POOL_WIDTH = POOL_GROUP * len(POOL_WINDOWS)
POOL_BUF = max(POOL_WINDOWS) - 1
HEAD_DIM = 64
V_DIM = 2 * HEAD_DIM
N_HEADS = 4
ATTN_WIDTH = N_HEADS * V_DIM
ROT_DIM = HEAD_DIM // 4
ROPE_THETA = 500000.0
EPS = 1e-6
PAGE_SIZE = 128
N_PROJ = 6
Q_SCALE = HEAD_DIM ** -0.5 * math.log2(math.e)

LANES = 128
SUBLANES = 8
HALO = 2 * SUBLANES
VMEM_LIMIT = 56 * 1024 * 1024

PROJ_ROWS = 512
FINISH_ROWS = 1024
ATTN_Q_SUB = 256
ATTN_K_SUB = 128
ATTN_MASKED_TILE_COST = 1.2
ATTN_INIT_COST = 3.0
ATTN_FINISH_COST = 11.0
DECODE_PAGES = 16
DECODE_SLOTS = 4
ATTN_RUNS = 2


def _lam_init(layer):
    return 0.8 - 0.6 * math.exp(-0.3 * layer)


def _lam(lq1_ref, lk1_ref, lq2_ref, lk2_ref, lam_init):
    a = jnp.sum(lq1_ref[...] * lk1_ref[...], axis=-1, keepdims=True)
    b = jnp.sum(lq2_ref[...] * lk2_ref[...], axis=-1, keepdims=True)
    return jnp.exp(a) - jnp.exp(b) + lam_init


def _silu(x):
    return x * jax.nn.sigmoid(x)


def _rms(x, g):
    return x * lax.rsqrt(jnp.mean(x * x, axis=-1, keepdims=True) + EPS) * g


def _rope_tables(pos):
    half = ROT_DIM // 2
    inv = ROPE_THETA ** (-(jnp.arange(half, dtype=F32) * 2.0) / ROT_DIM)
    ang = pos.astype(F32)[:, None] * inv[None, :]
    cos, sin = jnp.cos(ang), jnp.sin(ang)
    n = pos.shape[0]
    rest = HEAD_DIM - ROT_DIM
    zeros_h = jnp.zeros((n, half), F32)
    c = jnp.concatenate([cos, cos, jnp.ones((n, rest), F32)], axis=1)
    a = jnp.concatenate([-sin, zeros_h, jnp.zeros((n, rest), F32)], axis=1)
    b = jnp.concatenate([zeros_h, sin, jnp.zeros((n, rest), F32)], axis=1)
    rep = LANES // HEAD_DIM
    return tuple(jnp.tile(t, (1, rep)) for t in (c, a, b))


def _proj_kernel(x_ref, g_ref, w_ref, c_ref, a_ref, b_ref, *refs, for_prompt, n_prev):
    prev_k, prev_v = refs[:n_prev], refs[n_prev:2 * n_prev]
    u_ref, gp_ref, k_ref, v_ref, ga_ref, *q_refs = refs[2 * n_prev:]
    rows = x_ref.shape[0]
    xb = _rms(x_ref[...], g_ref[...]).astype(BF16)
    width = ATTN_WIDTH
    half = ROT_DIM // 2

    def mm(j):
        return jnp.dot(xb, w_ref[:, j * width:(j + 1) * width], preferred_element_type=F32)

    def rope(z, j):
        zc = z[:, j * LANES:(j + 1) * LANES]
        return (zc * c_ref[...] + pltpu.roll(zc, LANES - half, 1) * a_ref[...]
                + pltpu.roll(zc, half, 1) * b_ref[...])

    u_ref[...] = mm(0)
    gp_ref[...] = mm(1)
    q = mm(2)
    k = mm(3)
    v = mm(4)
    for p in range(n_prev):
        k_ref[p] = prev_k[p][...]
        v_ref[p] = prev_v[p][...]
    for j in range(N_HEADS):
        cols = slice(j * LANES, (j + 1) * LANES)
        qr = rope(q, j) * Q_SCALE
        kr = rope(k, j)
        if for_prompt:
            qt_ref, kb_ref, vt_ref = q_refs
            head_rows = pl.ds(j, rows, stride=N_HEADS)
            k_ref[n_prev, head_rows, :] = kr
            v_ref[n_prev, head_rows, :] = v[:, cols]
            qt_ref[cols, :] = qr.T.astype(BF16)
            kb_ref[:, cols] = kr.astype(BF16)
            vt_ref[cols, :] = v[:, cols].T.astype(BF16)
        else:
            k_ref[:, cols] = kr
            v_ref[:, cols] = v[:, cols]
            q_refs[0][:, cols] = qr
    ga_ref[...] = mm(5)


def _project(x, g_pre, w_in, tables, rows, for_prompt, prev_kv=()):
    t, d = x.shape
    n_blocks = t // rows
    n_tab = tables[0].shape[0] // rows
    n_prev = len(prev_kv) // 2
    width = ATTN_WIDTH
    row_spec = lambda w: pl.BlockSpec((rows, w), lambda i: (i, 0))
    tab_spec = pl.BlockSpec((rows, LANES), lambda i: (i % n_tab, 0))
    f32_out = jax.ShapeDtypeStruct((t, width), F32)
    if for_prompt:
        kv_spec = pl.BlockSpec((n_prev + 1, rows * N_HEADS, V_DIM), lambda i: (0, i, 0))
        kv_out = jax.ShapeDtypeStruct((n_prev + 1, t * N_HEADS, V_DIM), F32)
        t_spec = pl.BlockSpec((None, width, rows), lambda i: (i, 0, 0))
        t_out = jax.ShapeDtypeStruct((n_blocks, width, rows), BF16)
        out_specs = [row_spec(width)] * 2 + [kv_spec] * 2 + [row_spec(width), t_spec,
                                                              row_spec(width), t_spec]
        out_shape = [f32_out] * 2 + [kv_out] * 2 + [f32_out, t_out,
                                                    jax.ShapeDtypeStruct((t, width), BF16), t_out]
    else:
        out_specs = [row_spec(width)] * 6
        out_shape = [f32_out] * 6
    prev_spec = pl.BlockSpec((rows * N_HEADS, V_DIM), lambda i: (i, 0))
    return pl.pallas_call(
        functools.partial(_proj_kernel, for_prompt=for_prompt, n_prev=n_prev),
        grid=(n_blocks,),
        in_specs=[row_spec(d), pl.BlockSpec((1, d), lambda i: (0, 0)),
                  pl.BlockSpec((d, N_PROJ * width), lambda i: (0, 0),
                               pipeline_mode=pl.Buffered(1)),
                  tab_spec, tab_spec, tab_spec] + [prev_spec] * (2 * n_prev),
        out_specs=out_specs,
        out_shape=out_shape,
        compiler_params=pltpu.CompilerParams(dimension_semantics=("parallel",),
                                             vmem_limit_bytes=VMEM_LIMIT),
        name="proj",
    )(x, g_pre.reshape(1, d), w_in, *tables, *prev_kv)


def _attn_tiles(seq):
    tiles = []
    for kk in range(seq // ATTN_K_SUB):
        for half in range(2):
            for j in range(seq // ATTN_Q_SUB):
                first_key, last_key = kk * ATTN_K_SUB, (kk + 1) * ATTN_K_SUB - 1
                first_q, last_q = j * ATTN_Q_SUB, (j + 1) * ATTN_Q_SUB - 1
                if first_key <= last_q:
                    tiles.append((half, j, kk, last_key > first_q))
    return tiles


def _attn_phases(seq, n_phases):
    tiles = _attn_tiles(seq)
    cost = [ATTN_MASKED_TILE_COST if t[3] else 1.0 for t in tiles]
    budget = (ATTN_INIT_COST + sum(cost) + ATTN_FINISH_COST) / n_phases
    phases = [[] for _ in range(n_phases)]
    spent = ATTN_INIT_COST
    for t, c in zip(tiles, cost):
        phases[min(int((spent + c / 2) / budget), n_phases - 1)].append(t)
        spent += c
    return tuple(tuple(p) for p in phases)


def _attn_init(qt_ref, m_ref, l_ref, acc_ref, qh_ref):
    tile = qt_ref.shape[2]
    n_q = qh_ref.shape[0] // 2
    m_ref[...] = jnp.full(m_ref.shape, -jnp.inf, F32)
    l_ref[...] = jnp.zeros(l_ref.shape, F32)
    acc_ref[...] = jnp.zeros(acc_ref.shape, F32)
    feat = lax.broadcasted_iota(jnp.int32, (V_DIM, ATTN_Q_SUB), 0)
    for j in range(n_q):
        q0 = j * ATTN_Q_SUB
        qt = qt_ref[q0 // tile, :, q0 % tile:q0 % tile + ATTN_Q_SUB]
        qh_ref[j] = jnp.where(feat < HEAD_DIM, qt, jnp.zeros_like(qt))
        qh_ref[n_q + j] = jnp.where(feat >= HEAD_DIM, qt, jnp.zeros_like(qt))


def _attn_tile(half, j, kk, masked, k_ref, vt_ref, m_ref, l_ref, acc_ref, qh_ref):
    tile = vt_ref.shape[2]
    idx = half * (qh_ref.shape[0] // 2) + j
    k0 = kk * ATTN_K_SUB
    st = jnp.dot(k_ref[k0:k0 + ATTN_K_SUB, :], qh_ref[idx], preferred_element_type=F32)
    if masked:
        key = k0 + lax.broadcasted_iota(jnp.int32, st.shape, 0)
        qi = j * ATTN_Q_SUB + lax.broadcasted_iota(jnp.int32, st.shape, 1)
        st = jnp.where(key <= qi, st, -jnp.inf)
    m_old = m_ref[idx]
    m_new = jnp.maximum(m_old, jnp.max(st, axis=0, keepdims=True))
    alpha = jnp.exp2(m_old - m_new)
    pt = jnp.exp2(st - m_new)
    l_ref[idx] = alpha * l_ref[idx] + jnp.sum(pt, axis=0, keepdims=True)
    vt = vt_ref[k0 // tile, :, k0 % tile:k0 % tile + ATTN_K_SUB]
    acc_ref[idx] = alpha * acc_ref[idx] + jnp.dot(vt, pt.astype(BF16), preferred_element_type=F32)
    m_ref[idx] = m_new


def _attn_finish(lam, ga_ref, sub_ref, o_ref, l_ref, acc_ref, lam_init):
    n_q = acc_ref.shape[0] // 2
    for j in range(n_q):
        o = (acc_ref[j] / l_ref[j] - lam * (acc_ref[n_q + j] / l_ref[n_q + j])).T
        rows = slice(j * ATTN_Q_SUB, (j + 1) * ATTN_Q_SUB)
        y = _rms(o, sub_ref[...]) * (1.0 - lam_init)
        o_ref[rows, :] = (y * _silu(ga_ref[rows, :])).astype(BF16)


def _mix_out(x, pooled, gp_ref, attn_part, wmix_ref, ps_ref, wout_ref, gpost_ref):
    parts = []
    for g, pg in enumerate(pooled):
        cols = slice(g * POOL_GROUP, (g + 1) * POOL_GROUP)
        mixed = jnp.dot(pg.astype(BF16), wmix_ref[g], preferred_element_type=F32) * ps_ref[:, cols]
        parts.append((mixed * _silu(gp_ref[:, cols])).astype(BF16))
    mixed_all = jnp.concatenate(parts + [attn_part], axis=1)
    z = jnp.dot(mixed_all, wout_ref[...], preferred_element_type=F32)
    return x + _rms(z, gpost_ref[...])


def _window_sum(buf_ref, cols, w, rows):
    if w >= 2 * SUBLANES and w % (2 * SUBLANES) == 0:
        half = w // 2
        assert half + (half - 1) <= HALO
        part = buf_ref[HALO - half:HALO + rows, cols]
        for j in range(1, half):
            part = part + buf_ref[HALO - half - j:HALO + rows - j, cols]
        return part[half:] + part[:rows]
    acc = buf_ref[HALO:HALO + rows, cols]
    for j in range(1, w):
        acc = acc + buf_ref[HALO - j:HALO - j + rows, cols]
    return acc


def _finish_kernel(x_ref, u_ref, halo_ref, gp_ref, am_ref, wmix_ref, ps_ref, wout_ref, gpost_ref,
                   y_ref, buf_ref, *, rows, blocks_per_seq):
    blk_in_seq = pl.program_id(0) % blocks_per_seq
    buf_ref[0:HALO, :] = jnp.where(blk_in_seq == 0, 0.0, halo_ref[...])
    buf_ref[HALO:, :] = u_ref[...]
    pos = lax.broadcasted_iota(jnp.int32, (rows, 1), 0) + blk_in_seq * rows
    pooled = []
    for g, w in enumerate(POOL_WINDOWS):
        cols = slice(g * POOL_GROUP, (g + 1) * POOL_GROUP)
        tok = buf_ref[HALO:HALO + rows, cols]
        acc = _window_sum(buf_ref, cols, w, rows)
        cnt = jnp.minimum(w, pos + 1).astype(F32)
        pooled.append(acc / cnt - tok)
    y_ref[...] = _mix_out(x_ref[...], pooled, gp_ref, am_ref[...], wmix_ref, ps_ref, wout_ref,
                          gpost_ref)


def _prompt_finish(x, u, gp, am, w_mix, pool_scale, w_out, g_post, seq, rows):
    t, d = x.shape
    blocks_per_seq = seq // rows
    halo_per_block = rows // HALO
    row_spec = lambda w: pl.BlockSpec((rows, w), lambda i: (i, 0))
    halo_spec = pl.BlockSpec((HALO, POOL_WIDTH),
                             lambda i: (jnp.maximum(i * halo_per_block - 1, 0), 0))
    const = lambda shape: pl.BlockSpec(shape, lambda i: (0,) * len(shape))
    return pl.pallas_call(
        functools.partial(_finish_kernel, rows=rows, blocks_per_seq=blocks_per_seq),
        grid=(t // rows,),
        in_specs=[row_spec(d), row_spec(POOL_WIDTH), halo_spec, row_spec(POOL_WIDTH),
                  row_spec(ATTN_WIDTH), const(w_mix.shape), const((1, POOL_WIDTH)),
                  const(w_out.shape), const((1, d))],
        out_specs=row_spec(d),
        out_shape=jax.ShapeDtypeStruct((t, d), F32),
        scratch_shapes=[pltpu.VMEM((HALO + rows, POOL_WIDTH), F32)],
        compiler_params=pltpu.CompilerParams(dimension_semantics=("parallel",),
                                             vmem_limit_bytes=VMEM_LIMIT),
        name="prompt_finish",
    )(x, u, u, gp, am, w_mix, pool_scale.reshape(1, POOL_WIDTH), w_out, g_post.reshape(1, d))


def _sample_finish_kernel(x_ref, r_ref, gp_ref, ao_ref, ga_ref, sub_ref, wmix_ref, ps_ref,
                          wout_ref, gpost_ref, y_ref, *, n_new, counts, lam_init):
    pooled = []
    for g, w in enumerate(POOL_WINDOWS):
        cols = slice(g * POOL_GROUP, (g + 1) * POOL_GROUP)
        per_step = []
        for i in range(n_new):
            tok = r_ref[POOL_BUF + i, :, cols]
            acc = tok
            for j in range(1, w):
                acc = acc + r_ref[POOL_BUF + i - j, :, cols]
            per_step.append(acc / counts[g][i] - tok)
        pooled.append(jnp.concatenate(per_step, axis=0))
    heads = []
    for h in range(N_HEADS):
        cols = slice(h * V_DIM, (h + 1) * V_DIM)
        y = _rms(ao_ref[:, cols], sub_ref[...]) * (1.0 - lam_init)
        heads.append((y * _silu(ga_ref[:, cols])).astype(BF16))
    attn_part = jnp.concatenate(heads, axis=1)
    y_ref[...] = _mix_out(x_ref[...], pooled, gp_ref, attn_part, wmix_ref, ps_ref, wout_ref,
                          gpost_ref)


def _sample_finish(x, rows_hist, gp, ao, ga, subln, w_mix, pool_scale, w_out, g_post,
                   n_new, past_len, lam_init):
    t, d = x.shape
    counts = tuple(tuple(float(min(w, past_len + i + 1)) for i in range(n_new))
                   for w in POOL_WINDOWS)
    return pl.pallas_call(
        functools.partial(_sample_finish_kernel, n_new=n_new, counts=counts, lam_init=lam_init),
        out_shape=jax.ShapeDtypeStruct((t, d), F32),
        compiler_params=pltpu.CompilerParams(vmem_limit_bytes=VMEM_LIMIT),
        name="sample_finish",
    )(x, rows_hist, gp, ao, ga, subln.reshape(1, V_DIM), w_mix,
      pool_scale.reshape(1, POOL_WIDTH), w_out, g_post.reshape(1, d))


def _decode_step(t, total, steps_per_batch, lam, pt_ref, q_ref, kn_ref, vn_ref, ck_hbm, cv_hbm,
                 o_ref, kbuf, vbuf, sem, wq_ref, m_ref, l_ref, acc_ref, pad_ref,
                 *, n_new, pages, layer):
    j = t % steps_per_batch

    def copies(step, slot):
        out = []
        for r in range(pages):
            page = pt_ref[step * pages + r]
            out.append(pltpu.make_async_copy(ck_hbm.at[layer, page], kbuf.at[slot, r], sem.at[0, slot]))
            out.append(pltpu.make_async_copy(cv_hbm.at[layer, page], vbuf.at[slot, r], sem.at[1, slot]))
        return out

    @pl.when(t == 0)
    def _():
        for step in range(DECODE_SLOTS - 1):
            for c in copies(step, step):
                c.start()

    ahead = t + (DECODE_SLOTS - 1)

    @pl.when(ahead < total)
    def _():
        for n, c in enumerate(copies(ahead, ahead % DECODE_SLOTS)):
            c.start(priority=n % 2)

    slot = t % DECODE_SLOTS
    for c in copies(t, slot):
        c.wait()
    k_refs = [kbuf.at[slot, r] for r in range(pages)]
    v_refs = [vbuf.at[slot, r] for r in range(pages)]
    qh = n_new * N_HEADS
    page_rows = PAGE_SIZE * N_HEADS
    nt = (((1,), (1,)), ((), ()))
    row = lax.broadcasted_iota(jnp.int32, (2 * qh, page_rows), 0)
    col = lax.broadcasted_iota(jnp.int32, (2 * qh, page_rows), 1)
    same_head = (row % N_HEADS) == (col % N_HEADS)

    def update(state, scores, vals):
        m, l, acc = state
        m_new = m
        for s in scores:
            m_new = jnp.maximum(m_new, jnp.max(s, axis=-1, keepdims=True))
        alpha = jnp.exp2(m - m_new)
        l = alpha * l
        acc = alpha * acc
        for s, v in zip(scores, vals):
            p = jnp.exp2(s - m_new)
            l = l + jnp.sum(p, axis=-1, keepdims=True)
            acc = acc + jnp.dot(p.astype(BF16), v, preferred_element_type=F32)
        return m_new, l, acc

    def load_state():
        return m_ref[...], l_ref[...], acc_ref[...]

    def store_state(state):
        m_ref[...], l_ref[...], acc_ref[...] = state

    @pl.when(j == 0)
    def _():
        q = q_ref[...]
        lane = lax.broadcasted_iota(jnp.int32, q.shape, 1)
        wq_ref[...] = jnp.concatenate([jnp.where(lane < HEAD_DIM, q, 0.0),
                                       jnp.where(lane >= HEAD_DIM, q, 0.0)], axis=0).astype(BF16)
        pad_ref[...] = jnp.zeros(pad_ref.shape, F32)
        pad_ref[0:qh, :] = kn_ref[...]
        s = lax.dot_general(wq_ref[...], pad_ref[...].astype(BF16), nt, preferred_element_type=F32)
        causal = (col // N_HEADS) <= (row % qh) // N_HEADS
        s = jnp.where(same_head, jnp.where(causal, s, -jnp.inf), -jnp.inf)
        pad_ref[0:qh, :] = vn_ref[...]
        empty = (jnp.full(m_ref.shape, -jnp.inf, F32), jnp.zeros(l_ref.shape, F32),
                 jnp.zeros(acc_ref.shape, F32))
        store_state(update(empty, [s], [pad_ref[...].astype(BF16)]))

    wq = wq_ref[...]
    scores = [jnp.where(same_head,
                        lax.dot_general(wq, kr[...].astype(BF16), nt, preferred_element_type=F32),
                        -jnp.inf) for kr in k_refs]
    store_state(update(load_state(), scores, [vr[...].astype(BF16) for vr in v_refs]))

    @pl.when(j == steps_per_batch - 1)
    def _():
        n = acc_ref[...] / l_ref[...]
        o_ref[...] = n[:qh] - lam * n[qh:]


def _attention_kernel(pt_ref, q_ref, kn_ref, vn_ref, ck_hbm, cv_hbm, qt_ref, k_ref, vt_ref, ga_ref,
                      sub_ref, lq1_ref, lk1_ref, lq2_ref, lk2_ref, dec_ref, att_ref,
                      kbuf, vbuf, sem, wq_ref, dm_ref, dl_ref, dacc_ref, pad_ref,
                      am_ref, al_ref, aacc_ref, qh_ref,
                      *, n_new, pages, layer, lam_init, steps_per_batch, phases):
    phase = pl.program_id(1)
    n_phases = pl.num_programs(1)
    t = pl.program_id(0) * n_phases + phase
    lam = _lam(lq1_ref, lk1_ref, lq2_ref, lk2_ref, lam_init)
    _decode_step(t, pl.num_programs(0) * n_phases, steps_per_batch, lam, pt_ref, q_ref, kn_ref,
                 vn_ref, ck_hbm, cv_hbm, dec_ref, kbuf, vbuf, sem, wq_ref, dm_ref, dl_ref,
                 dacc_ref, pad_ref, n_new=n_new, pages=pages, layer=layer)

    busy = [p for p, tiles in enumerate(phases) if tiles]
    for p in busy:
        @pl.when(phase == p)
        def _(p=p):
            if p == busy[0]:
                _attn_init(qt_ref, am_ref, al_ref, aacc_ref, qh_ref)
            for tile in phases[p]:
                _attn_tile(*tile, k_ref, vt_ref, am_ref, al_ref, aacc_ref, qh_ref)
            if p == busy[-1]:
                _attn_finish(lam, ga_ref, sub_ref, att_ref, al_ref, aacc_ref, lam_init)


def _attention(q, k_new, v_new, cache_k, cache_v, page_table, qt, kb, vt, ga, subln, lams,
               layer, lam_init, batch, seq):
    db, qh, width = q.shape
    t = kb.shape[0]
    tile = qt.shape[2]
    pages = DECODE_PAGES
    page_rows = cache_k.shape[2]
    steps_per_batch = page_table.shape[1] // pages
    n_seqs = batch * N_HEADS
    n_phases = db * steps_per_batch // n_seqs
    assert n_phases * n_seqs == db * steps_per_batch and n_phases * n_seqs >= DECODE_SLOTS - 1
    stride = n_phases // ATTN_RUNS
    assert stride * ATTN_RUNS == n_phases
    runs = _attn_phases(seq, ATTN_RUNS)
    phases = tuple(runs[p // stride] if p % stride == 0 else () for p in range(n_phases))
    n_state = 2 * (seq // ATTN_Q_SUB)

    tok_spec = pl.BlockSpec((None, qh, width),
                            lambda g, p, pt: ((g * n_phases + p) // steps_per_batch, 0, 0))
    hbm_spec = pl.BlockSpec(memory_space=pl.ANY)
    seq_spec = pl.BlockSpec((seq, V_DIM), lambda g, p, pt: (g // N_HEADS, g % N_HEADS))
    t_spec = pl.BlockSpec((seq // tile, V_DIM, tile),
                          lambda g, p, pt: (g // N_HEADS, g % N_HEADS, 0))
    vec = lambda n: pl.BlockSpec((1, n), lambda g, p, pt: (0, 0))
    ring = pltpu.VMEM((DECODE_SLOTS, pages, page_rows, width), F32)
    grid_spec = pltpu.PrefetchScalarGridSpec(
        num_scalar_prefetch=1,
        grid=(n_seqs, n_phases),
        in_specs=([tok_spec] * 3 + [hbm_spec] * 2 + [t_spec, seq_spec, t_spec, seq_spec,
                                                     vec(V_DIM)] + [vec(HEAD_DIM)] * 4),
        out_specs=[tok_spec, seq_spec],
        scratch_shapes=[ring, ring, pltpu.SemaphoreType.DMA((2, DECODE_SLOTS)),
                        pltpu.VMEM((2 * qh, width), BF16),
                        pltpu.VMEM((2 * qh, 1), F32),
                        pltpu.VMEM((2 * qh, 1), F32),
                        pltpu.VMEM((2 * qh, width), F32),
                        pltpu.VMEM((page_rows, width), F32),
                        pltpu.VMEM((n_state, 1, ATTN_Q_SUB), F32),
                        pltpu.VMEM((n_state, 1, ATTN_Q_SUB), F32),
                        pltpu.VMEM((n_state, V_DIM, ATTN_Q_SUB), F32),
                        pltpu.VMEM((n_state, V_DIM, ATTN_Q_SUB), BF16)],
    )
    return pl.pallas_call(
        functools.partial(_attention_kernel, n_new=qh // N_HEADS, pages=pages, layer=layer,
                          lam_init=lam_init, steps_per_batch=steps_per_batch, phases=phases),
        grid_spec=grid_spec,
        out_shape=[jax.ShapeDtypeStruct((db, qh, width), F32),
                   jax.ShapeDtypeStruct((t, ATTN_WIDTH), BF16)],
        compiler_params=pltpu.CompilerParams(dimension_semantics=("arbitrary", "arbitrary"),
                                             vmem_limit_bytes=VMEM_LIMIT),
        name="attention",
    )(page_table.reshape(-1), q, k_new, v_new, cache_k, cache_v, qt, kb, vt, ga,
      subln.reshape(1, V_DIM), *lams)


def kernel(x_prompt, x_sample, cache_k, cache_v, state_pool, page_table, norm_pre, norm_post,
           w_in, w_pool_mix, pool_scale, lambda_q1, lambda_k1, lambda_q2, lambda_k2, subln, w_out):
    batch, seq, d = x_prompt.shape
    db, n_new, _ = x_sample.shape
    depth = w_in.shape[0]
    n_phys = cache_k.shape[1]
    past_len = page_table.shape[1] * PAGE_SIZE
    assert seq % PROJ_ROWS == 0 and seq % FINISH_ROWS == 0 and seq % ATTN_Q_SUB == 0 and page_table.shape[1] % DECODE_PAGES == 0
    assert w_in.shape[2] == N_PROJ * ATTN_WIDTH and POOL_WIDTH == ATTN_WIDTH

    w_in_b = w_in.astype(BF16)
    w_out_b = w_out.astype(BF16)
    w_mix_b = w_pool_mix.astype(BF16)
    cache_k = cache_k.reshape(depth, n_phys, PAGE_SIZE * N_HEADS, V_DIM)
    cache_v = cache_v.reshape(depth, n_phys, PAGE_SIZE * N_HEADS, V_DIM)
    tables_p = _rope_tables(jnp.arange(seq))
    tables_s = _rope_tables(past_len + jnp.repeat(jnp.arange(n_new), db))
    xp = x_prompt.reshape(batch * seq, d)
    xs = x_sample.transpose(1, 0, 2).reshape(n_new * db, d)

    def to_batch_major(a):
        return (a.reshape(n_new, db, N_HEADS, V_DIM).transpose(1, 0, 2, 3)
                .reshape(db, n_new * N_HEADS, V_DIM))

    outs = [[] for _ in range(4)]
    prev_k, prev_v = [], []
    for layer in range(depth):
        lam_init = _lam_init(layer)
        lams = [v[layer].reshape(1, HEAD_DIM) for v in (lambda_q1, lambda_k1, lambda_q2, lambda_k2)]
        prev_kv = prev_k + prev_v if layer == depth - 1 else []
        up, gpp, kp, vp, gap, qt, kb, vt = _project(xp, norm_pre[layer], w_in_b[layer], tables_p,
                                                    PROJ_ROWS, True, prev_kv)
        u, gp, k, v, ga, q = _project(xs, norm_pre[layer], w_in_b[layer], tables_s, n_new * db, False)
        ao, am = _attention(to_batch_major(q), to_batch_major(k), to_batch_major(v), cache_k,
                            cache_v, page_table, qt, kb, vt, gap, subln[layer], lams, layer,
                            lam_init, batch, seq)
        xp = _prompt_finish(xp, up, gpp, am, w_mix_b[layer], pool_scale[layer], w_out_b[layer],
                            norm_post[layer], seq, FINISH_ROWS)
        if layer == depth - 1:
            new_k_prompt = kp.reshape(depth, batch, seq, N_HEADS, V_DIM)
            new_v_prompt = vp.reshape(depth, batch, seq, N_HEADS, V_DIM)
        else:
            prev_k.append(kp[0])
            prev_v.append(vp[0])
        outs[0].append(up.reshape(batch, seq, POOL_WIDTH)[:, seq - POOL_BUF:])
        hist = jnp.concatenate([state_pool[layer].transpose(1, 0, 2),
                                u.reshape(n_new, db, POOL_WIDTH)], axis=0)
        ao = ao.reshape(db, n_new, ATTN_WIDTH).transpose(1, 0, 2).reshape(n_new * db, ATTN_WIDTH)
        xs = _sample_finish(xs, hist, gp, ao, ga,
                            subln[layer], w_mix_b[layer], pool_scale[layer], w_out_b[layer],
                            norm_post[layer], n_new, past_len, lam_init)
        outs[1].append(to_batch_major(k).reshape(db, n_new, N_HEADS, V_DIM))
        outs[2].append(to_batch_major(v).reshape(db, n_new, N_HEADS, V_DIM))
        outs[3].append(hist[n_new:].transpose(1, 0, 2))
    y_prompt = xp.reshape(batch, seq, d)
    y_sample = xs.reshape(n_new, db, d).transpose(1, 0, 2)
    new_pool_prompt, new_k_sample, new_v_sample, new_pool_sample = (jnp.stack(o) for o in outs)
    return (y_prompt, y_sample, new_k_prompt, new_v_prompt, new_pool_prompt,
            new_k_sample, new_v_sample, new_pool_sample)
```
